```python
import math
import jax, jax.numpy as jnp
from jax import lax
import numpy as np

D_MODEL = 1024
BATCH = 4
SEQ = 8192
DEPTH = 2

GRID_W = 64
ROPE_THETA = 10000.0
BLK = 128
N_BRANCH = 4
HEAD_DIM = 64
BRANCH_W = 512
LN_EPS = 1e-5
NEG_INF = -1e30

SWA_HEADS = 8
SWA_KV_HEADS = 2
SWA_WINDOW = 128
MLA_HEADS = 8
MLA_Q_RANK = 384
MLA_KV_RANK = 256
MLA_NOPE = 64
MLA_ROPE = 32
MLA_V = 64
DIFF_HEADS = 4
DIFF_QK = 64
DIFF_V = 128
NAT_HEADS = 8
NAT_KR_MAX = 8
NAT_KC = 16
PEER_HEADS = 8
PEER_NKEYS = 128
PEER_EXPERTS = PEER_NKEYS * PEER_NKEYS
PEER_DKEY = 256
PEER_TOPK = 16
PEER_TOK_BLK = 128

DN_ALPHA = (2 * DEPTH) ** 0.25
DN_BETA = (8 * DEPTH) ** -0.25

IN_WIDTHS = (
    SWA_HEADS * HEAD_DIM, SWA_KV_HEADS * HEAD_DIM, SWA_KV_HEADS * HEAD_DIM,
    MLA_Q_RANK, MLA_KV_RANK, MLA_ROPE,
    DIFF_HEADS * 2 * DIFF_QK, DIFF_HEADS * 2 * DIFF_QK, DIFF_HEADS * DIFF_V,
    NAT_HEADS * HEAD_DIM, NAT_HEADS * HEAD_DIM, NAT_HEADS * HEAD_DIM,
)
IN_SPLITS = tuple(int(v) for v in np.cumsum(IN_WIDTHS)[:-1])
D_IN = int(sum(IN_WIDTHS))

kernel_name = "hybrid_gated_parallel_mixers_peer_encoder"


def layer_norm(x, g=None, b=None):
    xf = x.astype(jnp.float32)
    mu = jnp.mean(xf, axis=-1, keepdims=True)
    var = jnp.mean(jnp.square(xf - mu), axis=-1, keepdims=True)
    y = (xf - mu) * lax.rsqrt(var + LN_EPS)
    if g is not None:
        y = y * g.astype(jnp.float32) + b.astype(jnp.float32)
    return y.astype(x.dtype)


def rms_norm(x, g):
    xf = x.astype(jnp.float32)
    y = xf * lax.rsqrt(jnp.mean(jnp.square(xf), axis=-1, keepdims=True) + LN_EPS)
    return (y * g.astype(jnp.float32)).astype(x.dtype)


def modulate(x, shift, scale):
    return layer_norm(x) * (1.0 + scale[:, None, :]) + shift[:, None, :]


def rope_tables(seq, dim):
    inv = ROPE_THETA ** (-jnp.arange(0, dim, 2, dtype=jnp.float32) / dim)
    ang = jnp.arange(seq, dtype=jnp.float32)[:, None] * inv[None, :]
    return jnp.cos(ang), jnp.sin(ang)


def apply_rope(t, cos, sin):
    t1, t2 = jnp.split(t, 2, axis=-1)
    cs = cos[None, :, None, :].astype(t.dtype)
    sn = sin[None, :, None, :].astype(t.dtype)
    return jnp.concatenate([t1 * cs - t2 * sn, t1 * sn + t2 * cs], axis=-1)


def qblock(a, i):
    return lax.dynamic_slice_in_dim(a, i * BLK, BLK, axis=1)


def unblock(out, b, s):
    return jnp.moveaxis(out, 0, 1).reshape(b, s, -1)


def swa_attention(q, k, v, sink, cos, sin):
    b, s, hq, d = q.shape
    hkv = k.shape[2]
    grp = hq // hkv
    q = apply_rope(q, cos, sin)
    k = apply_rope(k, cos, sin)
    pad = ((0, 0), (SWA_WINDOW, SWA_WINDOW), (0, 0), (0, 0))
    kp = jnp.pad(k, pad)
    vp = jnp.pad(v, pad)
    span = BLK + 2 * SWA_WINDOW
    sink_f = sink.astype(jnp.float32).reshape(hkv, grp)
    scale = d ** -0.5

    def block(i):
        s0 = i * BLK
        qb = qblock(q, i).reshape(b, BLK, hkv, grp, d)
        kb = lax.dynamic_slice_in_dim(kp, s0, span, axis=1)
        vb = lax.dynamic_slice_in_dim(vp, s0, span, axis=1)
        logits = jnp.einsum('bqkgd,bskd->bkgqs', qb, kb).astype(jnp.float32) * scale
        qpos = s0 + jnp.arange(BLK)
        kpos = s0 - SWA_WINDOW + jnp.arange(span)
        valid = ((jnp.abs(qpos[:, None] - kpos[None, :]) <= SWA_WINDOW)
                 & (kpos >= 0)[None, :] & (kpos < s)[None, :])
        logits = jnp.where(valid, logits, NEG_INF)
        sink_col = jnp.broadcast_to(sink_f[None, :, :, None, None], (b, hkv, grp, BLK, 1))
        p = jax.nn.softmax(jnp.concatenate([logits, sink_col], axis=-1), axis=-1)[..., :span]
        o = jnp.einsum('bkgqs,bskd->bqkgd', p.astype(v.dtype), vb)
        return o.reshape(b, BLK, hq * d)

    return unblock(lax.map(block, jnp.arange(s // BLK)), b, s)


def mla_attention(cq, ckv, krope, q_norm, q_up, kv_norm, kv_up, cos, sin):
    b, s, _ = cq.shape
    q = (rms_norm(cq, q_norm) @ q_up).reshape(b, s, MLA_HEADS, MLA_NOPE + MLA_ROPE)
    q_nope = q[..., :MLA_NOPE]
    q_rope = apply_rope(q[..., MLA_NOPE:], cos, sin)
    kv = (rms_norm(ckv, kv_norm) @ kv_up).reshape(b, s, MLA_HEADS, MLA_NOPE + MLA_V)
    k_nope = kv[..., :MLA_NOPE]
    v = kv[..., MLA_NOPE:]
    k_rope = apply_rope(krope[:, :, None, :], cos, sin)[:, :, 0]
    scale = (MLA_NOPE + MLA_ROPE) ** -0.5

    def block(i):
        logits = (jnp.einsum('bqhd,bshd->bhqs', qblock(q_nope, i), k_nope)
                  + jnp.einsum('bqhr,bsr->bhqs', qblock(q_rope, i), k_rope)).astype(jnp.float32) * scale
        p = jax.nn.softmax(logits, axis=-1)
        o = jnp.einsum('bhqs,bshd->bqhd', p.astype(v.dtype), v)
        return o.reshape(b, BLK, MLA_HEADS * MLA_V)

    return unblock(lax.map(block, jnp.arange(s // BLK)), b, s)


def diff_attention(q, k, v, lq1, lk1, lq2, lk2, subln, lambda_init, cos, sin):
    b, s, _ = q.shape
    q = apply_rope(q.reshape(b, s, DIFF_HEADS * 2, DIFF_QK), cos, sin).reshape(b, s, DIFF_HEADS, 2, DIFF_QK)
    k = apply_rope(k.reshape(b, s, DIFF_HEADS * 2, DIFF_QK), cos, sin).reshape(b, s, DIFF_HEADS, 2, DIFF_QK)
    v = v.reshape(b, s, DIFF_HEADS, DIFF_V)
    f32 = jnp.float32
    lam = (jnp.exp(jnp.sum(lq1.astype(f32) * lk1.astype(f32)))
           - jnp.exp(jnp.sum(lq2.astype(f32) * lk2.astype(f32))) + lambda_init)
    scale = DIFF_QK ** -0.5

    def block(i):
        logits = jnp.einsum('bqhmd,bshmd->bhmqs', qblock(q, i), k).astype(f32) * scale
        p = jax.nn.softmax(logits, axis=-1)
        a = p[:, :, 0] - lam * p[:, :, 1]
        o = jnp.einsum('bhqs,bshd->bqhd', a.astype(v.dtype), v)
        o = rms_norm(o, subln) * (1.0 - lambda_init)
        return o.reshape(b, BLK, DIFF_HEADS * DIFF_V)

    return unblock(lax.map(block, jnp.arange(s // BLK)), b, s)


def neighborhood_attention(q, k, v, rpb):
    b, s, _ = q.shape
    rows = s // GRID_W
    kr = min(NAT_KR_MAX, rows)
    grid = (b, rows, GRID_W, NAT_HEADS, HEAD_DIM)
    q = q.reshape(grid)
    k = k.reshape(grid)
    v = v.reshape(grid)
    cols = jnp.arange(GRID_W)
    col_start = jnp.clip(cols - NAT_KC // 2, 0, GRID_W - NAT_KC)
    col_idx = col_start[:, None] + jnp.arange(NAT_KC)[None, :]
    col_off = col_idx - cols[:, None] + (NAT_KC - 1)
    rpb_f = rpb.astype(jnp.float32)
    scale = HEAD_DIM ** -0.5

    def row(r):
        r0 = jnp.clip(r - kr // 2, 0, rows - kr)
        kband = lax.dynamic_slice_in_dim(k, r0, kr, axis=1)[:, :, col_idx]
        vband = lax.dynamic_slice_in_dim(v, r0, kr, axis=1)[:, :, col_idx]
        qr = lax.dynamic_index_in_dim(q, r, axis=1, keepdims=False)
        logits = jnp.einsum('bjhd,brjkhd->bhjrk', qr, kband).astype(jnp.float32) * scale
        row_off = r0 + jnp.arange(kr) - r + (NAT_KR_MAX - 1)
        bias = rpb_f[:, row_off[:, None, None], col_off[None, :, :]]
        logits = logits + jnp.transpose(bias, (0, 2, 1, 3))[None]
        p = jax.nn.softmax(logits.reshape(b, NAT_HEADS, GRID_W, kr * NAT_KC), axis=-1)
        p = p.reshape(b, NAT_HEADS, GRID_W, kr, NAT_KC)
        o = jnp.einsum('bhjrk,brjkhd->bjhd', p.astype(v.dtype), vband)
        return o.reshape(b, GRID_W, NAT_HEADS * HEAD_DIM)

    return unblock(lax.map(row, jnp.arange(rows)), b, s)


def peer_ffn(u, wq, keys, u_tab, v_tab):
    b, s, d = u.shape
    q = (u @ wq).reshape(b, s, PEER_HEADS, 2, PEER_DKEY // 2)
    scores = jnp.einsum('bshpk,hpnk->bshpn', q, keys).astype(jnp.float32)
    sub_s, sub_i = lax.top_k(scores, PEER_TOPK)
    cand_s = (sub_s[..., 0, :, None] + sub_s[..., 1, None, :]).reshape(b, s, PEER_HEADS, PEER_TOPK * PEER_TOPK)
    cand_i = (sub_i[..., 0, :, None] * PEER_NKEYS + sub_i[..., 1, None, :]).reshape(b, s, PEER_HEADS, PEER_TOPK * PEER_TOPK)
    top_s, pos = lax.top_k(cand_s, PEER_TOPK)
    eidx = jnp.take_along_axis(cand_i, pos, axis=-1)
    gates = jax.nn.softmax(top_s, axis=-1)
    nb = (b * s) // PEER_TOK_BLK
    ub = u.reshape(nb, PEER_TOK_BLK, d)
    ib = eidx.reshape(nb, PEER_TOK_BLK, PEER_HEADS, PEER_TOPK)
    gb = gates.reshape(nb, PEER_TOK_BLK, PEER_HEADS, PEER_TOPK)

    def blk(args):
        ut, it, gt = args
        ue = u_tab[it]
        h = jax.nn.gelu(jnp.einsum('td,thkd->thk', ut, ue).astype(jnp.float32), approximate=False)
        w = (gt * h).astype(ut.dtype)
        return jnp.einsum('thk,thkd->td', w, v_tab[it])

    return lax.map(blk, (ub, ib, gb)).reshape(b, s, d)


def setup_inputs(seed: int = 0) -> dict:
    key = jax.random.key(seed)
    ks = jax.random.split(key, 32)
    L, D = DEPTH, D_MODEL

    def nrm(k, shape, std):
        return std * jax.random.normal(k, shape, jnp.float32)

    def gain(k, shape):
        return 1.0 + 0.05 * jax.random.normal(k, shape, jnp.float32)

    return {
        "x": nrm(ks[0], (BATCH, SEQ, D), 1.0),
        "c": nrm(ks[1], (BATCH, D), 1.0),
        "ada_w": nrm(ks[2], (L, D, 6 * D), 0.5 * D ** -0.5),
        "ada_b": nrm(ks[3], (L, 6 * D), 0.01),
        "w_in": nrm(ks[4], (L, D, D_IN), D ** -0.5),
        "swa_sink": nrm(ks[5], (L, SWA_HEADS), 0.5),
        "mla_q_norm": gain(ks[6], (L, MLA_Q_RANK)),
        "mla_q_up": nrm(ks[7], (L, MLA_Q_RANK, MLA_HEADS * (MLA_NOPE + MLA_ROPE)), MLA_Q_RANK ** -0.5),
        "mla_kv_norm": gain(ks[8], (L, MLA_KV_RANK)),
        "mla_kv_up": nrm(ks[9], (L, MLA_KV_RANK, MLA_HEADS * (MLA_NOPE + MLA_V)), MLA_KV_RANK ** -0.5),
        "diff_lambda_q1": nrm(ks[10], (L, DIFF_QK), 0.1),
        "diff_lambda_k1": nrm(ks[11], (L, DIFF_QK), 0.1),
        "diff_lambda_q2": nrm(ks[12], (L, DIFF_QK), 0.1),
        "diff_lambda_k2": nrm(ks[13], (L, DIFF_QK), 0.1),
        "diff_subln": gain(ks[14], (L, DIFF_V)),
        "nat_rpb": nrm(ks[15], (L, NAT_HEADS, 2 * NAT_KR_MAX - 1, 2 * NAT_KC - 1), 0.1),
        "w_gate": nrm(ks[16], (L, N_BRANCH, D, D), D ** -0.5),
        "w_branch": nrm(ks[17], (L, N_BRANCH, BRANCH_W, D), DN_BETA * BRANCH_W ** -0.5),
        "w_out": nrm(ks[18], (L, D, D), DN_BETA * D ** -0.5),
        "ln1_g": gain(ks[19], (L, D)),
        "ln1_b": nrm(ks[20], (L, D), 0.02),
        "peer_wq": nrm(ks[21], (L, D, PEER_HEADS * PEER_DKEY), D ** -0.5),
        "peer_keys": nrm(ks[22], (L, PEER_HEADS, 2, PEER_NKEYS, PEER_DKEY // 2), (PEER_DKEY // 2) ** -0.5),
        "peer_u": nrm(ks[23], (L, PEER_EXPERTS, D), D ** -0.5),
        "peer_v": nrm(ks[24], (L, PEER_EXPERTS, D), DN_BETA),
        "ln2_g": gain(ks[25], (L, D)),
        "ln2_b": nrm(ks[26], (L, D), 0.02),
    }


def reference(x, c, ada_w, ada_b, w_in, swa_sink, mla_q_norm, mla_q_up, mla_kv_norm, mla_kv_up,
              diff_lambda_q1, diff_lambda_k1, diff_lambda_q2, diff_lambda_k2, diff_subln, nat_rpb,
              w_gate, w_branch, w_out, ln1_g, ln1_b, peer_wq, peer_keys, peer_u, peer_v, ln2_g, ln2_b):
    b, s, d = x.shape
    cos64, sin64 = rope_tables(s, HEAD_DIM)
    cos32, sin32 = rope_tables(s, MLA_ROPE)
    c_act = jax.nn.silu(c)
    for l in range(DEPTH):
        mod = c_act @ ada_w[l] + ada_b[l]
        sh_mix, sc_mix, g_mix, sh_ffn, sc_ffn, g_ffn = jnp.split(mod, 6, axis=-1)

        u = modulate(x, sh_mix, sc_mix)
        (qa, ka, va, cq, ckv, krope, qc, kc, vc, qd, kd, vd) = jnp.split(u @ w_in[l], IN_SPLITS, axis=-1)
        o_a = swa_attention(qa.reshape(b, s, SWA_HEADS, HEAD_DIM),
                            ka.reshape(b, s, SWA_KV_HEADS, HEAD_DIM),
                            va.reshape(b, s, SWA_KV_HEADS, HEAD_DIM), swa_sink[l], cos64, sin64)
        o_b = mla_attention(cq, ckv, krope, mla_q_norm[l], mla_q_up[l], mla_kv_norm[l], mla_kv_up[l], cos32, sin32)
        lambda_init = 0.8 - 0.6 * math.exp(-0.3 * l)
        o_c = diff_attention(qc, kc, vc, diff_lambda_q1[l], diff_lambda_k1[l], diff_lambda_q2[l],
                             diff_lambda_k2[l], diff_subln[l], lambda_init, cos64, sin64)
        o_d = neighborhood_attention(qd, kd, vd, nat_rpb[l])
        branches = (o_a, o_b, o_c, o_d)
        merged = jnp.zeros_like(x)
        for n in range(N_BRANCH):
            merged = merged + jax.nn.sigmoid(u @ w_gate[l, n]) * (branches[n] @ w_branch[l, n])
        y = merged @ w_out[l]
        x = layer_norm(DN_ALPHA * x + g_mix[:, None, :] * y, ln1_g[l], ln1_b[l])

        u = modulate(x, sh_ffn, sc_ffn)
        y = peer_ffn(u, peer_wq[l], peer_keys[l], peer_u[l], peer_v[l])
        x = layer_norm(DN_ALPHA * x + g_ffn[:, None, :] * y, ln2_g[l], ln2_b[l])
    return x
```

```python
import functools
import math

import jax
import jax.numpy as jnp
import numpy as np
from jax import lax
from jax.experimental import pallas as pl
from jax.experimental.pallas import tpu as pltpu

F32 = jnp.float32
BF16 = jnp.bfloat16

GRID_W = 64
ROPE_THETA = 10000.0
HEAD_DIM = 64
LN_EPS = 1e-5
NEG_INF = -1e30

SWA_HEADS = 8
SWA_KV_HEADS = 2
SWA_WINDOW = 128
MLA_HEADS = 8
MLA_NOPE = 64
MLA_ROPE = 32
MLA_V = 64
MLA_SLOT = 128
DIFF_HEADS = 4
DIFF_QK = 64
DIFF_V = 128
NAT_HEADS = 8
NAT_KR = 8
NAT_KC = 16
PEER_HEADS = 8
PEER_NKEYS = 128
PEER_DKEY = 256
PEER_TOPK = 16

VMEM_LIMIT_BYTES = 56 * 1024 * 1024

_MAIN_GROUPS = (("qa", 512), ("ka", 128), ("va", 128), ("cq", 384), ("ckv", 256), ("qc", 512), ("kc", 512),
                ("vc", 512), ("qd", 512), ("kd", 512), ("vd", 512), ("kr", 128))
_ROT_GROUPS = (("qa", 512), ("ka", 128), ("qc", 512), ("kc", 512), ("kr", 128))


def _offsets(groups, base=0):
    out, off = {}, base
    for name, width in groups:
        out[name] = (off, off + width)
        off += width
    return out, off


_MAIN_OFF, _MAIN_END = _offsets(_MAIN_GROUPS)
_ROT_OFF, _W_ALL_COLS = _offsets(_ROT_GROUPS, _MAIN_END)

_NT = (((1,), (1,)), ((), ()))


def _params(*sem):
    return pltpu.CompilerParams(dimension_semantics=sem, vmem_limit_bytes=VMEM_LIMIT_BYTES)


def _norm_rows(x):
    mu = jnp.mean(x, axis=-1, keepdims=True)
    xc = x - mu
    var = jnp.mean(xc * xc, axis=-1, keepdims=True)
    return xc * lax.rsqrt(var + LN_EPS)


def _sigmoid(z):
    return 1.0 / (1.0 + jnp.exp(-z))


def _bdot(a, b):
    return jnp.dot(a, b, preferred_element_type=F32)


def _ada_kernel(c_ref, w_ref, b_ref, o_ref):
    c = c_ref[...]
    act = c * _sigmoid(c)
    o_ref[...] = jnp.dot(act, w_ref[...], preferred_element_type=F32,
                         precision=lax.Precision.HIGHEST) + b_ref[...]


def _ada(c, ada_w, ada_b):
    depth, d, six_d = ada_w.shape
    b = c.shape[0]
    rows = -(-b // 8) * 8
    c_pad = jnp.pad(c, ((0, rows - b), (0, 0)))
    out = pl.pallas_call(
        _ada_kernel,
        grid=(depth, six_d // d),
        in_specs=[pl.BlockSpec((rows, d), lambda l, j: (0, 0)),
                  pl.BlockSpec((None, d, d), lambda l, j: (l, 0, j)),
                  pl.BlockSpec((None, 1, d), lambda l, j: (l, 0, j))],
        out_specs=pl.BlockSpec((None, rows, d), lambda l, j: (l, 0, j)),
        out_shape=jax.ShapeDtypeStruct((depth, rows, six_d), F32),
        compiler_params=_params("parallel", "parallel"),
        name="ada",
    )(c_pad, ada_w, ada_b.reshape(depth, 1, six_d))
    return out[:, :b].reshape(depth, b, 6, 1, d)


def _in_proj_kernel(x_ref, sh_ref, sc_ref, w_ref, cos_ref, sin_ref, cos32_ref, sin32_ref,
                    qa_ref, ka_ref, va_ref, cq_ref, ckv_ref, kr_ref, qc_ref, kc_ref, vc_ref,
                    qd_ref, kd_ref, vd_ref):
    u = _norm_rows(x_ref[...]) * (1.0 + sc_ref[...]) + sh_ref[...]
    ub = u.astype(BF16)

    def proj(off):
        return _bdot(ub, w_ref[:, off[0]:off[1]])

    def roped(name, cos, sin):
        main, rot = proj(_MAIN_OFF[name]), proj(_ROT_OFF[name])
        reps = main.shape[1] // cos.shape[1]
        if reps > 1:
            cos = jnp.concatenate([cos] * reps, axis=1)
            sin = jnp.concatenate([sin] * reps, axis=1)
        return main * cos + rot * sin

    cos, sin = cos_ref[...], sin_ref[...]
    qa_ref[...] = roped("qa", cos, sin).astype(qa_ref.dtype)
    ka_ref[...] = roped("ka", cos, sin).astype(ka_ref.dtype)
    va_ref[...] = proj(_MAIN_OFF["va"]).astype(va_ref.dtype)
    cq_ref[...] = proj(_MAIN_OFF["cq"])
    ckv_ref[...] = proj(_MAIN_OFF["ckv"])
    kr_ref[...] = roped("kr", cos32_ref[...], sin32_ref[...]).astype(kr_ref.dtype)
    qc_ref[...] = roped("qc", cos, sin).astype(qc_ref.dtype)
    kc_ref[...] = roped("kc", cos, sin).astype(kc_ref.dtype)
    vc_ref[...] = proj(_MAIN_OFF["vc"]).astype(vc_ref.dtype)
    qd_ref[...] = proj(_MAIN_OFF["qd"]).astype(qd_ref.dtype)
    kd_ref[...] = proj(_MAIN_OFF["kd"]).astype(kd_ref.dtype)
    vd_ref[...] = proj(_MAIN_OFF["vd"]).astype(vd_ref.dtype)


def _in_proj(x2, shift, scale, w_all, tabs, seq, tm):
    n, d = x2.shape
    spt = seq // tm
    widths = dict(_MAIN_GROUPS)
    names = ("qa", "ka", "va", "cq", "ckv", "kr", "qc", "kc", "vc", "qd", "kd", "vd")
    dtypes = {k: BF16 for k in names}
    dtypes["cq"] = F32
    dtypes["ckv"] = F32
    row = lambda i: (i, 0)
    mod = lambda i: (i // spt, 0, 0)
    tab = lambda i: (i % spt, 0)
    return pl.pallas_call(
        _in_proj_kernel,
        grid=(n // tm,),
        in_specs=[pl.BlockSpec((tm, d), row),
                  pl.BlockSpec((None, 1, d), mod),
                  pl.BlockSpec((None, 1, d), mod),
                  pl.BlockSpec((d, _W_ALL_COLS), lambda i: (0, 0), pipeline_mode=pl.Buffered(1)),
                  pl.BlockSpec((tm, 128), tab), pl.BlockSpec((tm, 128), tab),
                  pl.BlockSpec((tm, 128), tab), pl.BlockSpec((tm, 128), tab)],
        out_specs=[pl.BlockSpec((tm, widths[k]), row) for k in names],
        out_shape=[jax.ShapeDtypeStruct((n, widths[k]), dtypes[k]) for k in names],
        compiler_params=_params("parallel"),
        name="in_proj",
    )(x2, shift, scale, w_all, tabs["cos64"], tabs["sin64"], tabs["cos32"], tabs["sin32"])


def _swa_kernel(q_ref, k_ref, v_ref, sink_ref, o_ref, *, tq, seq):
    i = pl.program_id(1)
    span = tq + 2 * SWA_WINDOW
    start = pl.multiple_of(jnp.clip(i * tq - SWA_WINDOW, 0, seq - span), SWA_WINDOW)
    ks = k_ref[pl.ds(start, span), :]
    vs = v_ref[pl.ds(start, span), :]
    qpos = i * tq + lax.broadcasted_iota(jnp.int32, (tq, span), 0)
    kpos = start + lax.broadcasted_iota(jnp.int32, (tq, span), 1)
    valid = jnp.abs(qpos - kpos) <= SWA_WINDOW
    q = q_ref[...]
    grp = SWA_HEADS // SWA_KV_HEADS
    outs = []
    for h in range(SWA_HEADS):
        kv = h // grp
        qh = q[:, h * HEAD_DIM:(h + 1) * HEAD_DIM]
        kh = ks[:, kv * HEAD_DIM:(kv + 1) * HEAD_DIM]
        vh = vs[:, kv * HEAD_DIM:(kv + 1) * HEAD_DIM]
        s = lax.dot_general(qh, kh, _NT, preferred_element_type=F32)
        s = jnp.where(valid, s, NEG_INF)
        sink = sink_ref[h][:, :1]
        m = jnp.maximum(jnp.max(s, axis=-1, keepdims=True), sink)
        p = jnp.exp(s - m)
        denom = jnp.sum(p, axis=-1, keepdims=True) + jnp.exp(sink - m)
        outs.append(_bdot(p.astype(BF16), vh) / denom)
    o_ref[...] = jnp.concatenate(outs, axis=1).astype(o_ref.dtype)


def _swa(qa, ka, va, sink, batch, seq, tq):
    n = qa.shape[0]
    nq = seq // tq
    sink_b = jnp.broadcast_to(sink.astype(F32)[:, None, None], (SWA_HEADS, 1, 128))
    return pl.pallas_call(
        functools.partial(_swa_kernel, tq=tq, seq=seq),
        grid=(batch, nq),
        in_specs=[pl.BlockSpec((tq, 512), lambda b, i: (b * nq + i, 0)),
                  pl.BlockSpec((seq, 128), lambda b, i: (b, 0)),
                  pl.BlockSpec((seq, 128), lambda b, i: (b, 0)),
                  pl.BlockSpec((SWA_HEADS, 1, 128), lambda b, i: (0, 0, 0))],
        out_specs=pl.BlockSpec((tq, 512), lambda b, i: (b * nq + i, 0)),
        out_shape=jax.ShapeDtypeStruct((n, 512), BF16),
        compiler_params=_params("parallel", "parallel"),
        name="swa",
    )(qa, ka, va, sink_b)


def _mla_prep_kernel(cq_ref, ckv_ref, kr_ref, qn_ref, kvn_ref, wq_ref, wqr_ref, wk_ref, e_ref, wv_ref,
                     ct_ref, st_ref, qm_ref, km_ref, vm_ref):
    def rms(x, g):
        return (x * lax.rsqrt(jnp.mean(x * x, axis=-1, keepdims=True) + LN_EPS) * g).astype(BF16)

    qn = rms(cq_ref[...], qn_ref[...])
    kvn = rms(ckv_ref[...], kvn_ref[...])
    cos = jnp.concatenate([ct_ref[...]] * MLA_HEADS, axis=1)
    sin = jnp.concatenate([st_ref[...]] * MLA_HEADS, axis=1)
    qm_ref[...] = (_bdot(qn, wq_ref[...]) * cos + _bdot(qn, wqr_ref[...]) * sin).astype(qm_ref.dtype)
    km_ref[...] = (_bdot(kvn, wk_ref[...]) + _bdot(kr_ref[...], e_ref[...])).astype(km_ref.dtype)
    vm_ref[...] = _bdot(kvn, wv_ref[...]).astype(vm_ref.dtype)


def _mla_prep(cq, ckv, kr, w, tabs, seq, tm):
    n = cq.shape[0]
    spt = seq // tm
    row = lambda i: (i, 0)
    full = lambda i: (0, 0)
    tab = lambda i: (i % spt, 0)
    hw = MLA_HEADS * MLA_SLOT
    return pl.pallas_call(
        _mla_prep_kernel,
        grid=(n // tm,),
        in_specs=[pl.BlockSpec((tm, cq.shape[1]), row), pl.BlockSpec((tm, ckv.shape[1]), row),
                  pl.BlockSpec((tm, 128), row),
                  pl.BlockSpec((1, cq.shape[1]), full), pl.BlockSpec((1, ckv.shape[1]), full),
                  pl.BlockSpec(w["wq"].shape, full), pl.BlockSpec(w["wqr"].shape, full),
                  pl.BlockSpec(w["wk"].shape, full), pl.BlockSpec(w["e"].shape, full),
                  pl.BlockSpec(w["wv"].shape, full),
                  pl.BlockSpec((tm, 128), tab), pl.BlockSpec((tm, 128), tab)],
        out_specs=[pl.BlockSpec((tm, hw), row), pl.BlockSpec((tm, hw), row),
                   pl.BlockSpec((tm, MLA_HEADS * MLA_V), row)],
        out_shape=[jax.ShapeDtypeStruct((n, hw), BF16), jax.ShapeDtypeStruct((n, hw), BF16),
                   jax.ShapeDtypeStruct((n, MLA_HEADS * MLA_V), BF16)],
        compiler_params=_params("parallel"),
        name="mla_prep",
    )(cq, ckv, kr, w["qn"], w["kvn"], w["wq"], w["wqr"], w["wk"], w["e"], w["wv"],
      tabs["mla_cos"], tabs["mla_sin"])


def _flash_rows(q, k_ref, v_ref, k_lanes, seq, tk, scale):
    rows = q.shape[0]
    dv = v_ref.shape[1]

    def step(kt, carry):
        m, l, acc = carry
        off = pl.multiple_of(kt * tk, tk)
        ks = k_ref[pl.ds(off, tk), k_lanes[0]:k_lanes[1]]
        vs = v_ref[pl.ds(off, tk), :]
        s = lax.dot_general(q, ks, _NT, preferred_element_type=F32)
        if scale != 1.0:
            s = s * scale
        m_new = jnp.maximum(m, jnp.max(s, axis=-1, keepdims=True))
        alpha = jnp.exp(m - m_new)
        p = jnp.exp(s - m_new)
        l = alpha * l + jnp.sum(p, axis=-1, keepdims=True)
        acc = alpha * acc + _bdot(p.astype(BF16), vs)
        return m_new, l, acc

    init = (jnp.full((rows, 1), NEG_INF, F32), jnp.zeros((rows, 1), F32), jnp.zeros((rows, dv), F32))
    _, l, acc = lax.fori_loop(0, seq // tk, step, init)
    return acc, l


def _mla_kernel(q_ref, k_ref, v_ref, o_ref, *, seq, tk):
    scale = (MLA_NOPE + MLA_ROPE) ** -0.5
    outs = []
    for hh in range(2):
        lanes = (hh * MLA_SLOT, (hh + 1) * MLA_SLOT)
        acc, l = _flash_rows(q_ref[:, lanes[0]:lanes[1]], k_ref, v_ref, lanes, seq, tk, scale)
        outs.append(acc / l)
    lane = lax.broadcasted_iota(jnp.int32, outs[0].shape, 1)
    o_ref[...] = jnp.where(lane < MLA_V, outs[0], outs[1]).astype(o_ref.dtype)


def _mla(qm, km, vm, batch, seq, tq, tk):
    n = qm.shape[0]
    nq = seq // tq
    pairs = MLA_HEADS // 2
    return pl.pallas_call(
        functools.partial(_mla_kernel, seq=seq, tk=tk),
        grid=(batch, pairs, nq),
        in_specs=[pl.BlockSpec((tq, 2 * MLA_SLOT), lambda b, h, i: (b * nq + i, h)),
                  pl.BlockSpec((seq, 2 * MLA_SLOT), lambda b, h, i: (b, h)),
                  pl.BlockSpec((seq, 2 * MLA_V), lambda b, h, i: (b, h))],
        out_specs=pl.BlockSpec((tq, 2 * MLA_V), lambda b, h, i: (b * nq + i, h)),
        out_shape=jax.ShapeDtypeStruct((n, MLA_HEADS * MLA_V), BF16),
        compiler_params=_params("parallel", "parallel", "parallel"),
        name="mla",
    )(qm, km, vm)


def _diff_kernel(q_ref, k_ref, v_ref, lq1_ref, lk1_ref, lq2_ref, lk2_ref, sub_ref, o_ref, *,
                 seq, tk, lambda_init):
    q = q_ref[...]
    tq = q.shape[0]
    lane = lax.broadcasted_iota(jnp.int32, q.shape, 1)
    zero = jnp.zeros_like(q)
    q_both = jnp.concatenate([jnp.where(lane < DIFF_QK, q, zero), jnp.where(lane >= DIFF_QK, q, zero)], axis=0)
    acc, l = _flash_rows(q_both, k_ref, v_ref, (0, 2 * DIFF_QK), seq, tk, 1.0)
    o = acc / l
    lam = (jnp.exp(jnp.sum(lq1_ref[...] * lk1_ref[...], axis=-1, keepdims=True))
           - jnp.exp(jnp.sum(lq2_ref[...] * lk2_ref[...], axis=-1, keepdims=True)) + lambda_init)
    o = o[:tq] - lam * o[tq:]
    o = o * lax.rsqrt(jnp.mean(o * o, axis=-1, keepdims=True) + LN_EPS) * sub_ref[...]
    o_ref[...] = (o * (1.0 - lambda_init)).astype(o_ref.dtype)


def _diff(qc, kc, vc, lams, subln, lambda_init, batch, seq, tq, tk):
    n = qc.shape[0]
    nq = seq // tq
    vec = lambda b, h, i: (0, 0)
    return pl.pallas_call(
        functools.partial(_diff_kernel, seq=seq, tk=tk, lambda_init=lambda_init),
        grid=(batch, DIFF_HEADS, nq),
        in_specs=[pl.BlockSpec((tq, 2 * DIFF_QK), lambda b, h, i: (b * nq + i, h)),
                  pl.BlockSpec((seq, 2 * DIFF_QK), lambda b, h, i: (b, h)),
                  pl.BlockSpec((seq, DIFF_V), lambda b, h, i: (b, h)),
                  pl.BlockSpec((1, DIFF_QK), vec), pl.BlockSpec((1, DIFF_QK), vec),
                  pl.BlockSpec((1, DIFF_QK), vec), pl.BlockSpec((1, DIFF_QK), vec),
                  pl.BlockSpec((1, DIFF_V), vec)],
        out_specs=pl.BlockSpec((tq, DIFF_V), lambda b, h, i: (b * nq + i, h)),
        out_shape=jax.ShapeDtypeStruct((n, DIFF_HEADS * DIFF_V), BF16),
        compiler_params=_params("parallel", "parallel", "parallel"),
        name="diff",
    )(qc, kc, vc, *lams, subln)


def _nat_row_start(r, rows):
    return jnp.clip(r - NAT_KR // 2, 0, rows - NAT_KR)


def _nat_kernel(q_ref, k_ref, v_ref, bias_ref, o_ref, *, rows):
    r0 = _nat_row_start(pl.program_id(1), rows)
    off = pl.multiple_of(r0 * GRID_W, GRID_W)
    ks = k_ref[pl.ds(off, NAT_KR * GRID_W), :]
    vs = v_ref[pl.ds(off, NAT_KR * GRID_W), :]
    q = q_ref[...]
    outs = []
    for h in range(NAT_HEADS):
        sl = slice(h * HEAD_DIM, (h + 1) * HEAD_DIM)
        s = lax.dot_general(q[:, sl], ks[:, sl], _NT, preferred_element_type=F32) + bias_ref[h]
        m = jnp.max(s, axis=-1, keepdims=True)
        p = jnp.exp(s - m)
        outs.append(_bdot(p.astype(BF16), vs[:, sl]) / jnp.sum(p, axis=-1, keepdims=True))
    o_ref[...] = jnp.concatenate(outs, axis=1).astype(o_ref.dtype)


def _nat_bias_table(rpb):
    cols = np.arange(GRID_W)
    col_start = np.clip(cols - NAT_KC // 2, 0, GRID_W - NAT_KC)
    kc = np.arange(GRID_W)
    inside = (kc[None, :] >= col_start[:, None]) & (kc[None, :] < col_start[:, None] + NAT_KC)
    col_off = np.clip(kc[None, :] - cols[:, None] + (NAT_KC - 1), 0, 2 * NAT_KC - 2)
    row_off = np.arange(NAT_KR)[:, None] + np.arange(NAT_KR)[None, :]
    tab = rpb.astype(F32)[:, row_off[:, :, None, None], col_off[None, None, :, :]]
    tab = jnp.where(inside[None, None, None], tab, NEG_INF)
    tab = jnp.transpose(tab, (1, 0, 3, 2, 4))
    return tab.reshape(NAT_KR, NAT_HEADS, GRID_W, NAT_KR * GRID_W)


def _nat(qd, kd, vd, rpb, batch, seq):
    n = qd.shape[0]
    rows = seq // GRID_W
    assert rows >= NAT_KR
    bias = _nat_bias_table(rpb)

    def bias_idx(b, r):
        return (_nat_row_start(r, rows) - r + (NAT_KR - 1), 0, 0, 0)

    return pl.pallas_call(
        functools.partial(_nat_kernel, rows=rows),
        grid=(batch, rows),
        in_specs=[pl.BlockSpec((GRID_W, 512), lambda b, r: (b * rows + r, 0)),
                  pl.BlockSpec((seq, 512), lambda b, r: (b, 0)),
                  pl.BlockSpec((seq, 512), lambda b, r: (b, 0)),
                  pl.BlockSpec((None, NAT_HEADS, GRID_W, NAT_KR * GRID_W), bias_idx)],
        out_specs=pl.BlockSpec((GRID_W, 512), lambda b, r: (b * rows + r, 0)),
        out_shape=jax.ShapeDtypeStruct((n, 512), BF16),
        compiler_params=_params("parallel", "arbitrary"),
        name="nat",
    )(qd, kd, vd, bias)


def _merge_kernel(x_ref, oa_ref, ob_ref, oc_ref, od_ref, sh_ref, sc_ref, gm_ref, wg_ref, wb_ref, wo_ref,
                  g_ref, b_ref, o_ref, *, alpha):
    x = x_ref[...]
    ub = (_norm_rows(x) * (1.0 + sc_ref[...]) + sh_ref[...]).astype(BF16)
    merged = None
    for n, br_ref in enumerate((oa_ref, ob_ref, oc_ref, od_ref)):
        term = _sigmoid(_bdot(ub, wg_ref[n])) * _bdot(br_ref[...], wb_ref[n])
        merged = term if merged is None else merged + term
    y = _bdot(merged.astype(BF16), wo_ref[...])
    z = alpha * x + gm_ref[...] * y
    o_ref[...] = _norm_rows(z) * g_ref[...] + b_ref[...]


def _merge(x2, branches, shift, scale, gate, wg, wb, wo, ln_g, ln_b, alpha, seq, tm):
    n, d = x2.shape
    spt = seq // tm
    row = lambda i: (i, 0)
    mod = lambda i: (i // spt, 0, 0)
    once = pl.Buffered(1)
    return pl.pallas_call(
        functools.partial(_merge_kernel, alpha=alpha),
        grid=(n // tm,),
        in_specs=[pl.BlockSpec((tm, d), row)] + [pl.BlockSpec((tm, 512), row)] * 4
                 + [pl.BlockSpec((None, 1, d), mod)] * 3
                 + [pl.BlockSpec(wg.shape, lambda i: (0, 0, 0), pipeline_mode=once),
                    pl.BlockSpec(wb.shape, lambda i: (0, 0, 0), pipeline_mode=once),
                    pl.BlockSpec(wo.shape, lambda i: (0, 0), pipeline_mode=once),
                    pl.BlockSpec((1, d), lambda i: (0, 0)), pl.BlockSpec((1, d), lambda i: (0, 0))],
        out_specs=pl.BlockSpec((tm, d), row),
        out_shape=jax.ShapeDtypeStruct((n, d), F32),
        compiler_params=_params("parallel"),
        name="merge",
    )(x2, *branches, shift, scale, gate, wg, wb, wo, ln_g, ln_b)


def _take_top(w, count, on_max):
    for r in range(count):
        m = jnp.max(w, axis=0, keepdims=True)
        on_max(r, m)
        if r + 1 < count:
            w = jnp.where(w == m, -jnp.inf, w)


def _peer_kernel(x_ref, sh_ref, sc_ref, gf_ref, g_ref, b_ref, wqt_ref, keys_ref, u_ref, vt_ref, o_ref,
                 ub_scr, q_scr, s_scr, p1_scr, p2_scr, thr_scr, top_scr, cand_scr, w_scr, acc_scr, *,
                 alpha, ipc):
    c = pl.program_id(1)
    nk = PEER_NKEYS
    k = PEER_TOPK

    @pl.when(c == 0)
    def _scores_and_thresholds():
        ub = (_norm_rows(x_ref[...]) * (1.0 + sc_ref[...]) + sh_ref[...]).astype(BF16)
        ub_scr[...] = ub
        q_scr[...] = lax.dot_general(wqt_ref[...], ub, _NT, preferred_element_type=F32).astype(BF16)

        def head(h, carry):
            for p in range(2):
                idx = 2 * h + p
                rows = pl.multiple_of(idx * nk, nk)
                s = _bdot(keys_ref[idx], q_scr[pl.ds(rows, nk), :])
                s_scr[idx] = s

                def keep(r, m, p=p):
                    top_scr[p, r:r + 1, :] = m

                _take_top(s, k + 1, keep)
            a = top_scr[0, :k, :]
            b = top_scr[1, :k, :]
            for i in range(k):
                cand_scr[i * k:(i + 1) * k, :] = a[i:i + 1, :] + b
            best = a[0:1, :] + b[0:1, :]
            stats = {"z": jnp.zeros_like(best)}

            def tally(r, m):
                if r < k:
                    stats["z"] = stats["z"] + jnp.exp(m - best)
                if r == k - 1:
                    stats["last"] = m
                if r == k:
                    stats["next"] = m

            _take_top(cand_scr[...], k + 1, tally)
            runner_up = jnp.maximum(stats["next"], jnp.maximum(top_scr[0, k:k + 1, :] + b[0:1, :],
                                                               a[0:1, :] + top_scr[1, k:k + 1, :]))
            thr_scr[pl.ds(h, 1), :] = 0.5 * (stats["last"] + runner_up)
            p1_scr[h] = jnp.exp(s_scr[2 * h] - a[0:1, :])
            p2_scr[h] = jnp.exp(s_scr[2 * h + 1] - b[0:1, :]) / stats["z"]
            return carry

        lax.fori_loop(0, PEER_HEADS, head, 0)
        acc_scr[...] = jnp.zeros_like(acc_scr)

    hid = lax.dot_general(u_ref[...], ub_scr[...], _NT, preferred_element_type=F32)
    for il in range(ipc):
        i = c * ipc + il
        gate = None
        for h in range(PEER_HEADS):
            s1_row = s_scr[2 * h, pl.ds(i, 1), :]
            p1_row = p1_scr[h, pl.ds(i, 1), :]
            chosen = s_scr[2 * h + 1] >= thr_scr[h:h + 1, :] - s1_row
            term = jnp.where(chosen, p2_scr[h], 0.0) * p1_row
            gate = term if gate is None else gate + term
        hb = hid[il * nk:(il + 1) * nk, :]
        act = 0.5 * hb * (1.0 + lax.erf(hb * (2.0 ** -0.5)))
        w_scr[il * nk:(il + 1) * nk, :] = (gate * act).astype(BF16)
    acc_scr[...] += _bdot(vt_ref[...], w_scr[...])

    @pl.when(c == pl.num_programs(1) - 1)
    def _finish():
        y = jnp.transpose(acc_scr[...])
        z = alpha * x_ref[...] + gf_ref[...] * y
        o_ref[...] = _norm_rows(z) * g_ref[...] + b_ref[...]


def _peer(x2, shift, scale, gate, ln_g, ln_b, wqt, keys, u_tab, vt_tab, alpha, seq, tt, chunk):
    n, d = x2.shape
    spt = seq // tt
    experts = u_tab.shape[0]
    ipc = chunk // PEER_NKEYS
    nhp = 2 * PEER_HEADS
    row = lambda i, c: (i, 0)
    mod = lambda i, c: (i // spt, 0, 0)
    return pl.pallas_call(
        functools.partial(_peer_kernel, alpha=alpha, ipc=ipc),
        grid=(n // tt, experts // chunk),
        in_specs=[pl.BlockSpec((tt, d), row)] + [pl.BlockSpec((None, 1, d), mod)] * 3
                 + [pl.BlockSpec((1, d), lambda i, c: (0, 0)), pl.BlockSpec((1, d), lambda i, c: (0, 0)),
                    pl.BlockSpec(wqt.shape, lambda i, c: (0, 0), pipeline_mode=pl.Buffered(1)),
                    pl.BlockSpec(keys.shape, lambda i, c: (0, 0, 0), pipeline_mode=pl.Buffered(1)),
                    pl.BlockSpec((chunk, d), lambda i, c: (c, 0)),
                    pl.BlockSpec((d, chunk), lambda i, c: (0, c))],
        out_specs=pl.BlockSpec((tt, d), row),
        out_shape=jax.ShapeDtypeStruct((n, d), F32),
        scratch_shapes=[pltpu.VMEM((tt, d), BF16),
                        pltpu.VMEM((nhp * PEER_NKEYS, tt), BF16),
                        pltpu.VMEM((nhp, PEER_NKEYS, tt), F32),
                        pltpu.VMEM((PEER_HEADS, PEER_NKEYS, tt), F32),
                        pltpu.VMEM((PEER_HEADS, PEER_NKEYS, tt), F32),
                        pltpu.VMEM((PEER_HEADS, tt), F32),
                        pltpu.VMEM((2, PEER_TOPK + 8, tt), F32),
                        pltpu.VMEM((PEER_TOPK * PEER_TOPK, tt), F32),
                        pltpu.VMEM((chunk, tt), BF16),
                        pltpu.VMEM((d, tt), F32)],
        compiler_params=_params("parallel", "arbitrary"),
        name="peer",
    )(x2, shift, scale, gate, ln_g, ln_b, wqt, keys, u_tab, vt_tab)


def _rotate_half_cols(w, dim):
    rows, cols = w.shape
    w3 = w.reshape(rows, cols // dim, dim)
    return jnp.concatenate([-w3[..., dim // 2:], w3[..., :dim // 2]], axis=-1).reshape(rows, cols)


def _pad_cols(w, width):
    return jnp.pad(w, ((0, 0), (0, width - w.shape[1])))


def _in_proj_weights(w_in):
    widths = (512, 128, 128, 384, 256, 32, 512, 512, 512, 512, 512, 512)
    names = ("qa", "ka", "va", "cq", "ckv", "kr", "qc", "kc", "vc", "qd", "kd", "vd")
    parts, off = {}, 0
    for name, width in zip(names, widths):
        parts[name] = w_in[:, off:off + width]
        off += width
    for name in ("qa", "qc", "qd"):
        parts[name] = parts[name] * (HEAD_DIM ** -0.5)
    rot_dim = {"qa": HEAD_DIM, "ka": HEAD_DIM, "qc": DIFF_QK, "kc": DIFF_QK, "kr": MLA_ROPE}
    cols = [_pad_cols(parts[name], width) for name, width in _MAIN_GROUPS]
    cols += [_pad_cols(_rotate_half_cols(parts[name], rot_dim[name]), width) for name, width in _ROT_GROUPS]
    return jnp.concatenate(cols, axis=1).astype(BF16)


def _mla_weights(q_norm, q_up, kv_norm, kv_up):
    qr, kvr = q_up.shape[0], kv_up.shape[0]
    qh = q_up.reshape(qr, MLA_HEADS, MLA_NOPE + MLA_ROPE)
    pad = MLA_SLOT - MLA_NOPE - MLA_ROPE
    wq = jnp.pad(qh, ((0, 0), (0, 0), (0, pad))).reshape(qr, MLA_HEADS * MLA_SLOT)
    rot = _rotate_half_cols(qh[..., MLA_NOPE:].reshape(qr, MLA_HEADS * MLA_ROPE), MLA_ROPE)
    rot = rot.reshape(qr, MLA_HEADS, MLA_ROPE)
    wqr = jnp.pad(rot, ((0, 0), (0, 0), (MLA_NOPE, pad))).reshape(qr, MLA_HEADS * MLA_SLOT)
    kvh = kv_up.reshape(kvr, MLA_HEADS, MLA_NOPE + MLA_V)
    wk = jnp.pad(kvh[..., :MLA_NOPE], ((0, 0), (0, 0), (0, MLA_SLOT - MLA_NOPE))).reshape(kvr, MLA_HEADS * MLA_SLOT)
    wv = kvh[..., MLA_NOPE:].reshape(kvr, MLA_HEADS * MLA_V)
    place = np.zeros((128, MLA_HEADS, MLA_SLOT), np.float32)
    for r in range(MLA_ROPE):
        place[r, :, MLA_NOPE + r] = 1.0
    return {"qn": q_norm.reshape(1, qr), "kvn": kv_norm.reshape(1, kvr),
            "wq": wq.astype(BF16), "wqr": wqr.astype(BF16), "wk": wk.astype(BF16), "wv": wv.astype(BF16),
            "e": jnp.asarray(place.reshape(128, MLA_HEADS * MLA_SLOT), BF16)}


def _rope_tables(seq):
    def base(dim):
        inv = ROPE_THETA ** (-jnp.arange(0, dim, 2, dtype=F32) / dim)
        ang = jnp.arange(seq, dtype=F32)[:, None] * inv[None, :]
        cos, sin = jnp.cos(ang), jnp.sin(ang)
        return jnp.concatenate([cos, cos], axis=1), jnp.concatenate([sin, sin], axis=1)

    cos64, sin64 = base(HEAD_DIM)
    cos32, sin32 = base(MLA_ROPE)
    ones = jnp.ones((seq, MLA_NOPE), F32)
    zeros = jnp.zeros((seq, MLA_NOPE), F32)
    tail = jnp.zeros((seq, MLA_SLOT - MLA_NOPE - MLA_ROPE), F32)
    return {"cos64": jnp.concatenate([cos64, cos64], axis=1), "sin64": jnp.concatenate([sin64, sin64], axis=1),
            "cos32": _pad_cols(cos32, 128), "sin32": _pad_cols(sin32, 128),
            "mla_cos": jnp.concatenate([ones, cos32, tail], axis=1),
            "mla_sin": jnp.concatenate([zeros, sin32, tail], axis=1)}


def _tile(total, want):
    t = min(total, want)
    assert total % t == 0
    return t


def kernel(x, c, ada_w, ada_b, w_in, swa_sink, mla_q_norm, mla_q_up, mla_kv_norm, mla_kv_up, diff_lambda_q1, diff_lambda_k1, diff_lambda_q2, diff_lambda_k2, diff_subln, nat_rpb, w_gate, w_branch, w_out, ln1_g, ln1_b, peer_wq, peer_keys, peer_u, peer_v, ln2_g, ln2_b):
    batch, seq, d = x.shape
    depth = ada_w.shape[0]
    alpha = (2 * depth) ** 0.25
    n = batch * seq
    tabs = _rope_tables(seq)
    mod = _ada(c, ada_w, ada_b)
    tm = _tile(seq, 512)
    x2 = x.reshape(n, d)
    for l in range(depth):
        sh_mix, sc_mix, g_mix, sh_ffn, sc_ffn, g_ffn = (mod[l, :, j] for j in range(6))
        (qa, ka, va, cq, ckv, kr, qc, kc, vc, qd, kd, vd) = _in_proj(
            x2, sh_mix, sc_mix, _in_proj_weights(w_in[l]), tabs, seq, tm)
        o_a = _swa(qa, ka, va, swa_sink[l], batch, seq, _tile(seq, 256))
        qm, km, vm = _mla_prep(cq, ckv, kr, _mla_weights(mla_q_norm[l], mla_q_up[l], mla_kv_norm[l], mla_kv_up[l]),
                               tabs, seq, tm)
        o_b = _mla(qm, km, vm, batch, seq, _tile(seq, 512), _tile(seq, 512))
        lambda_init = 0.8 - 0.6 * math.exp(-0.3 * l)
        lams = tuple(v[l].reshape(1, DIFF_QK) for v in (diff_lambda_q1, diff_lambda_k1, diff_lambda_q2, diff_lambda_k2))
        o_c = _diff(qc, kc, vc, lams, diff_subln[l].reshape(1, DIFF_V), lambda_init, batch, seq,
                    _tile(seq, 256), _tile(seq, 512))
        o_d = _nat(qd, kd, vd, nat_rpb[l], batch, seq)
        x2 = _merge(x2, (o_a, o_b, o_c, o_d), sh_mix, sc_mix, g_mix, w_gate[l].astype(BF16),
                    w_branch[l].astype(BF16), w_out[l].astype(BF16), ln1_g[l].reshape(1, d), ln1_b[l].reshape(1, d),
                    alpha, seq, tm)
        keys = peer_keys[l].reshape(2 * PEER_HEADS, PEER_NKEYS, PEER_DKEY // 2).astype(BF16)
        x2 = _peer(x2, sh_ffn, sc_ffn, g_ffn, ln2_g[l].reshape(1, d), ln2_b[l].reshape(1, d),
                   peer_wq[l].T.astype(BF16), keys, peer_u[l].astype(BF16), peer_v[l].T.astype(BF16),
                   alpha, seq, tm, 1024)
    return x2.reshape(batch, seq, d)
```

```python
import functools
import math

import jax
import jax.numpy as jnp
import numpy as np
from jax import lax
from jax.experimental import pallas as pl
from jax.experimental.pallas import tpu as pltpu

F32 = jnp.float32
BF16 = jnp.bfloat16

GRID_W = 64
ROPE_THETA = 10000.0
HEAD_DIM = 64
LN_EPS = 1e-5
NEG_INF = -1e30
LOG2_E = 1.4426950408889634

SWA_HEADS = 8
SWA_KV_HEADS = 2
SWA_WINDOW = 128
MLA_HEADS = 8
MLA_NOPE = 64
MLA_ROPE = 32
MLA_V = 64
MLA_SLOT = 128
DIFF_HEADS = 4
DIFF_QK = 64
DIFF_V = 128
NAT_HEADS = 8
NAT_KR = 8
NAT_KC = 16
PEER_HEADS = 8
PEER_NKEYS = 128
PEER_DKEY = 256
PEER_TOPK = 16

VMEM_LIMIT_BYTES = 56 * 1024 * 1024
FLASH_TQ = 512
FLASH_TK = 1024
NAT_ROWS_PER_STEP = 4
PEER_CHUNK = 2048
PEER_SUB = 512

_MAIN_GROUPS = (("qa", 512), ("ka", 128), ("va", 128), ("cq", 384), ("ckv", 256), ("qc", 512), ("kc", 512),
                ("vc", 512), ("qd", 512), ("kd", 512), ("vd", 512), ("kr", 128))
_ROT_GROUPS = (("qa", 512), ("ka", 128), ("qc", 512), ("kc", 512), ("kr", 128))


def _offsets(groups, base=0):
    out, off = {}, base
    for name, width in groups:
        out[name] = (off, off + width)
        off += width
    return out, off


_MAIN_OFF, _MAIN_END = _offsets(_MAIN_GROUPS)
_ROT_OFF, _W_ALL_COLS = _offsets(_ROT_GROUPS, _MAIN_END)

_NT = (((1,), (1,)), ((), ()))


def _params(*sem):
    return pltpu.CompilerParams(dimension_semantics=sem, vmem_limit_bytes=VMEM_LIMIT_BYTES)


def _norm_rows(x):
    mu = jnp.mean(x, axis=-1, keepdims=True)
    xc = x - mu
    var = jnp.mean(xc * xc, axis=-1, keepdims=True)
    return xc * lax.rsqrt(var + LN_EPS)


def _sigmoid(z):
    return 1.0 / (1.0 + jnp.exp(-z))


def _bdot(a, b):
    return jnp.dot(a, b, preferred_element_type=F32)


def _ada_kernel(c_ref, w_ref, b_ref, o_ref):
    c = c_ref[...]
    act = c * _sigmoid(c)
    o_ref[...] = jnp.dot(act, w_ref[...], preferred_element_type=F32,
                         precision=lax.Precision.HIGHEST) + b_ref[...]


def _ada(c, ada_w, ada_b):
    depth, d, six_d = ada_w.shape
    b = c.shape[0]
    rows = -(-b // 8) * 8
    c_pad = jnp.pad(c, ((0, rows - b), (0, 0)))
    out = pl.pallas_call(
        _ada_kernel,
        grid=(depth, six_d // d),
        in_specs=[pl.BlockSpec((rows, d), lambda l, j: (0, 0)),
                  pl.BlockSpec((None, d, d), lambda l, j: (l, 0, j)),
                  pl.BlockSpec((None, 1, d), lambda l, j: (l, 0, j))],
        out_specs=pl.BlockSpec((None, rows, d), lambda l, j: (l, 0, j)),
        out_shape=jax.ShapeDtypeStruct((depth, rows, six_d), F32),
        compiler_params=_params("parallel", "parallel"),
        name="ada",
    )(c_pad, ada_w, ada_b.reshape(depth, 1, six_d))
    return out[:, :b].reshape(depth, b, 6, 1, d)


def _in_proj_kernel(x_ref, sh_ref, sc_ref, w_ref, cos_ref, sin_ref, cos32_ref, sin32_ref,
                    qa_ref, ka_ref, va_ref, cq_ref, ckv_ref, kr_ref, qc_ref, kc_ref, vc_ref,
                    qd_ref, kd_ref, vd_ref):
    u = _norm_rows(x_ref[...]) * (1.0 + sc_ref[...]) + sh_ref[...]
    ub = u.astype(BF16)

    def proj(off):
        return _bdot(ub, w_ref[:, off[0]:off[1]])

    def roped(name, cos, sin):
        main, rot = proj(_MAIN_OFF[name]), proj(_ROT_OFF[name])
        reps = main.shape[1] // cos.shape[1]
        if reps > 1:
            cos = jnp.concatenate([cos] * reps, axis=1)
            sin = jnp.concatenate([sin] * reps, axis=1)
        return main * cos + rot * sin

    cos, sin = cos_ref[...], sin_ref[...]
    qa_ref[...] = roped("qa", cos, sin).astype(qa_ref.dtype)
    ka_ref[...] = roped("ka", cos, sin).astype(ka_ref.dtype)
    va_ref[...] = proj(_MAIN_OFF["va"]).astype(va_ref.dtype)
    cq_ref[...] = proj(_MAIN_OFF["cq"])
    ckv_ref[...] = proj(_MAIN_OFF["ckv"])
    kr_ref[...] = roped("kr", cos32_ref[...], sin32_ref[...]).astype(kr_ref.dtype)
    qc_ref[...] = roped("qc", cos, sin).astype(qc_ref.dtype)
    kc_ref[...] = roped("kc", cos, sin).astype(kc_ref.dtype)
    vc_ref[...] = proj(_MAIN_OFF["vc"]).astype(vc_ref.dtype)
    qd_ref[...] = proj(_MAIN_OFF["qd"]).astype(qd_ref.dtype)
    kd_ref[...] = proj(_MAIN_OFF["kd"]).astype(kd_ref.dtype)
    vd_ref[...] = proj(_MAIN_OFF["vd"]).astype(vd_ref.dtype)


def _in_proj(x2, shift, scale, w_all, tabs, seq, tm):
    n, d = x2.shape
    spt = seq // tm
    widths = dict(_MAIN_GROUPS)
    names = ("qa", "ka", "va", "cq", "ckv", "kr", "qc", "kc", "vc", "qd", "kd", "vd")
    dtypes = {k: BF16 for k in names}
    dtypes["cq"] = F32
    dtypes["ckv"] = F32
    row = lambda i: (i, 0)
    mod = lambda i: (i // spt, 0, 0)
    tab = lambda i: (i % spt, 0)
    return pl.pallas_call(
        _in_proj_kernel,
        grid=(n // tm,),
        in_specs=[pl.BlockSpec((tm, d), row),
                  pl.BlockSpec((None, 1, d), mod),
                  pl.BlockSpec((None, 1, d), mod),
                  pl.BlockSpec((d, _W_ALL_COLS), lambda i: (0, 0), pipeline_mode=pl.Buffered(1)),
                  pl.BlockSpec((tm, 128), tab), pl.BlockSpec((tm, 128), tab),
                  pl.BlockSpec((tm, 128), tab), pl.BlockSpec((tm, 128), tab)],
        out_specs=[pl.BlockSpec((tm, widths[k]), row) for k in names],
        out_shape=[jax.ShapeDtypeStruct((n, widths[k]), dtypes[k]) for k in names],
        compiler_params=_params("parallel"),
        name="in_proj",
    )(x2, shift, scale, w_all, tabs["cos64"], tabs["sin64"], tabs["cos32"], tabs["sin32"])


def _swa_kernel(q_ref, k_ref, v_ref, sink_ref, o_ref, *, tq, seq):
    i = pl.program_id(1)
    span = tq + 2 * SWA_WINDOW
    start = pl.multiple_of(jnp.clip(i * tq - SWA_WINDOW, 0, seq - span), SWA_WINDOW)
    ks = k_ref[pl.ds(start, span), :]
    vs = v_ref[pl.ds(start, span), :]
    qpos = i * tq + lax.broadcasted_iota(jnp.int32, (tq, span), 0)
    kpos = start + lax.broadcasted_iota(jnp.int32, (tq, span), 1)
    valid = jnp.abs(qpos - kpos) <= SWA_WINDOW
    q = q_ref[...]
    grp = SWA_HEADS // SWA_KV_HEADS
    outs = []
    for h in range(SWA_HEADS):
        kv = h // grp
        qh = q[:, h * HEAD_DIM:(h + 1) * HEAD_DIM]
        kh = ks[:, kv * HEAD_DIM:(kv + 1) * HEAD_DIM]
        vh = vs[:, kv * HEAD_DIM:(kv + 1) * HEAD_DIM]
        s = lax.dot_general(qh, kh, _NT, preferred_element_type=F32)
        s = jnp.where(valid, s, NEG_INF)
        sink = sink_ref[h][:, :1]
        m = jnp.maximum(jnp.max(s, axis=-1, keepdims=True), sink)
        p = jnp.exp(s - m)
        denom = jnp.sum(p, axis=-1, keepdims=True) + jnp.exp(sink - m)
        outs.append(_bdot(p.astype(BF16), vh) / denom)
    o_ref[...] = jnp.concatenate(outs, axis=1).astype(o_ref.dtype)


def _swa(qa, ka, va, sink, batch, seq, tq):
    n = qa.shape[0]
    nq = seq // tq
    sink_b = jnp.broadcast_to(sink.astype(F32)[:, None, None], (SWA_HEADS, 1, 128))
    return pl.pallas_call(
        functools.partial(_swa_kernel, tq=tq, seq=seq),
        grid=(batch, nq),
        in_specs=[pl.BlockSpec((tq, 512), lambda b, i: (b * nq + i, 0)),
                  pl.BlockSpec((seq, 128), lambda b, i: (b, 0)),
                  pl.BlockSpec((seq, 128), lambda b, i: (b, 0)),
                  pl.BlockSpec((SWA_HEADS, 1, 128), lambda b, i: (0, 0, 0))],
        out_specs=pl.BlockSpec((tq, 512), lambda b, i: (b * nq + i, 0)),
        out_shape=jax.ShapeDtypeStruct((n, 512), BF16),
        compiler_params=_params("parallel", "parallel"),
        name="swa",
    )(qa, ka, va, sink_b)


def _mla_prep_kernel(cq_ref, ckv_ref, kr_ref, qn_ref, kvn_ref, wq_ref, wqr_ref, wk_ref, e_ref, wv_ref,
                     ct_ref, st_ref, qm_ref, km_ref, vm_ref):
    def rms(x, g):
        return (x * lax.rsqrt(jnp.mean(x * x, axis=-1, keepdims=True) + LN_EPS) * g).astype(BF16)

    qn = rms(cq_ref[...], qn_ref[...])
    kvn = rms(ckv_ref[...], kvn_ref[...])
    cos = jnp.concatenate([ct_ref[...]] * MLA_HEADS, axis=1)
    sin = jnp.concatenate([st_ref[...]] * MLA_HEADS, axis=1)
    qm_ref[...] = (_bdot(qn, wq_ref[...]) * cos + _bdot(qn, wqr_ref[...]) * sin).astype(qm_ref.dtype)
    km_ref[...] = (_bdot(kvn, wk_ref[...]) + _bdot(kr_ref[...], e_ref[...])).astype(km_ref.dtype)
    vm_ref[...] = _bdot(kvn, wv_ref[...]).astype(vm_ref.dtype)


def _mla_prep(cq, ckv, kr, w, tabs, seq, tm):
    n = cq.shape[0]
    spt = seq // tm
    row = lambda i: (i, 0)
    full = lambda i: (0, 0)
    tab = lambda i: (i % spt, 0)
    hw = MLA_HEADS * MLA_SLOT
    return pl.pallas_call(
        _mla_prep_kernel,
        grid=(n // tm,),
        in_specs=[pl.BlockSpec((tm, cq.shape[1]), row), pl.BlockSpec((tm, ckv.shape[1]), row),
                  pl.BlockSpec((tm, 128), row),
                  pl.BlockSpec((1, cq.shape[1]), full), pl.BlockSpec((1, ckv.shape[1]), full),
                  pl.BlockSpec(w["wq"].shape, full), pl.BlockSpec(w["wqr"].shape, full),
                  pl.BlockSpec(w["wk"].shape, full), pl.BlockSpec(w["e"].shape, full),
                  pl.BlockSpec(w["wv"].shape, full),
                  pl.BlockSpec((tm, 128), tab), pl.BlockSpec((tm, 128), tab)],
        out_specs=[pl.BlockSpec((tm, hw), row), pl.BlockSpec((tm, hw), row),
                   pl.BlockSpec((tm, MLA_HEADS * MLA_V), row)],
        out_shape=[jax.ShapeDtypeStruct((n, hw), BF16), jax.ShapeDtypeStruct((n, hw), BF16),
                   jax.ShapeDtypeStruct((n, MLA_HEADS * MLA_V), BF16)],
        compiler_params=_params("parallel"),
        name="mla_prep",
    )(cq, ckv, kr, w["qn"], w["kvn"], w["wq"], w["wqr"], w["wk"], w["e"], w["wv"],
      tabs["mla_cos"], tabs["mla_sin"])


def _flash_chains(qs, k_ref, v_ref, k_lanes, seq, tk):
    dv = v_ref.shape[1]

    def step(kt, carry):
        off = pl.multiple_of(kt * tk, tk)
        vs = v_ref[pl.ds(off, tk), :]
        out = []
        for q, lanes, (m, l, acc) in zip(qs, k_lanes, carry):
            ks = k_ref[pl.ds(off, tk), lanes[0]:lanes[1]]
            s = lax.dot_general(q, ks, _NT, preferred_element_type=F32)
            m_new = jnp.maximum(m, jnp.max(s, axis=-1, keepdims=True))
            alpha = jnp.exp2(m - m_new)
            p = jnp.exp2(s - m_new)
            l = alpha * l + jnp.sum(p, axis=-1, keepdims=True)
            acc = alpha * acc + _bdot(p.astype(BF16), vs)
            out.append((m_new, l, acc))
        return tuple(out)

    init = tuple((jnp.full((q.shape[0], 1), NEG_INF, F32), jnp.zeros((q.shape[0], 1), F32),
                  jnp.zeros((q.shape[0], dv), F32)) for q in qs)
    final = lax.fori_loop(0, seq // tk, step, init)
    return [(acc, l) for _, l, acc in final]


def _mla_kernel(q_ref, k_ref, v_ref, o_ref, *, seq, tk):
    lanes = [(hh * MLA_SLOT, (hh + 1) * MLA_SLOT) for hh in range(2)]
    qs = [q_ref[:, lo:hi] for lo, hi in lanes]
    outs = [acc / l for acc, l in _flash_chains(qs, k_ref, v_ref, lanes, seq, tk)]
    lane = lax.broadcasted_iota(jnp.int32, outs[0].shape, 1)
    o_ref[...] = jnp.where(lane < MLA_V, outs[0], outs[1]).astype(o_ref.dtype)


def _mla(qm, km, vm, batch, seq, tq, tk):
    n = qm.shape[0]
    nq = seq // tq
    pairs = MLA_HEADS // 2
    return pl.pallas_call(
        functools.partial(_mla_kernel, seq=seq, tk=tk),
        grid=(batch, pairs, nq),
        in_specs=[pl.BlockSpec((tq, 2 * MLA_SLOT), lambda b, h, i: (b * nq + i, h)),
                  pl.BlockSpec((seq, 2 * MLA_SLOT), lambda b, h, i: (b, h)),
                  pl.BlockSpec((seq, 2 * MLA_V), lambda b, h, i: (b, h))],
        out_specs=pl.BlockSpec((tq, 2 * MLA_V), lambda b, h, i: (b * nq + i, h)),
        out_shape=jax.ShapeDtypeStruct((n, MLA_HEADS * MLA_V), BF16),
        compiler_params=_params("parallel", "parallel", "parallel"),
        name="mla",
    )(qm, km, vm)


def _diff_kernel(q_ref, k_ref, v_ref, lq1_ref, lk1_ref, lq2_ref, lk2_ref, sub_ref, o_ref, *,
                 seq, tk, lambda_init):
    q = q_ref[...]
    lane = lax.broadcasted_iota(jnp.int32, q.shape, 1)
    zero = jnp.zeros_like(q)
    qs = [jnp.where(lane < DIFF_QK, q, zero), jnp.where(lane >= DIFF_QK, q, zero)]
    (acc1, l1), (acc2, l2) = _flash_chains(qs, k_ref, v_ref, [(0, 2 * DIFF_QK)] * 2, seq, tk)
    lam = (jnp.exp(jnp.sum(lq1_ref[...] * lk1_ref[...], axis=-1, keepdims=True))
           - jnp.exp(jnp.sum(lq2_ref[...] * lk2_ref[...], axis=-1, keepdims=True)) + lambda_init)
    o = acc1 / l1 - lam * (acc2 / l2)
    o = o * lax.rsqrt(jnp.mean(o * o, axis=-1, keepdims=True) + LN_EPS) * sub_ref[...]
    o_ref[...] = (o * (1.0 - lambda_init)).astype(o_ref.dtype)


def _diff(qc, kc, vc, lams, subln, lambda_init, batch, seq, tq, tk):
    n = qc.shape[0]
    nq = seq // tq
    vec = lambda b, h, i: (0, 0)
    return pl.pallas_call(
        functools.partial(_diff_kernel, seq=seq, tk=tk, lambda_init=lambda_init),
        grid=(batch, DIFF_HEADS, nq),
        in_specs=[pl.BlockSpec((tq, 2 * DIFF_QK), lambda b, h, i: (b * nq + i, h)),
                  pl.BlockSpec((seq, 2 * DIFF_QK), lambda b, h, i: (b, h)),
                  pl.BlockSpec((seq, DIFF_V), lambda b, h, i: (b, h)),
                  pl.BlockSpec((1, DIFF_QK), vec), pl.BlockSpec((1, DIFF_QK), vec),
                  pl.BlockSpec((1, DIFF_QK), vec), pl.BlockSpec((1, DIFF_QK), vec),
                  pl.BlockSpec((1, DIFF_V), vec)],
        out_specs=pl.BlockSpec((tq, DIFF_V), lambda b, h, i: (b * nq + i, h)),
        out_shape=jax.ShapeDtypeStruct((n, DIFF_HEADS * DIFF_V), BF16),
        compiler_params=_params("parallel", "parallel", "parallel"),
        name="diff",
    )(qc, kc, vc, *lams, subln)


def _nat_row_start(r, rows):
    return jnp.clip(r - NAT_KR // 2, 0, rows - NAT_KR)


def _nat_kernel(q_ref, k_ref, v_ref, bias_ref, o_ref, *, rows, rows_per_step):
    for rr in range(rows_per_step):
        r = pl.program_id(1) * rows_per_step + rr
        r0 = _nat_row_start(r, rows)
        band = r0 - r + (NAT_KR - 1)
        off = pl.multiple_of(r0 * GRID_W, GRID_W)
        ks = k_ref[pl.ds(off, NAT_KR * GRID_W), :]
        vs = v_ref[pl.ds(off, NAT_KR * GRID_W), :]
        q = q_ref[rr * GRID_W:(rr + 1) * GRID_W, :]
        outs = []
        for h in range(NAT_HEADS):
            sl = slice(h * HEAD_DIM, (h + 1) * HEAD_DIM)
            s = lax.dot_general(q[:, sl], ks[:, sl], _NT, preferred_element_type=F32) + bias_ref[band, h]
            m = jnp.max(s, axis=-1, keepdims=True)
            p = jnp.exp(s - m)
            outs.append(_bdot(p.astype(BF16), vs[:, sl]) / jnp.sum(p, axis=-1, keepdims=True))
        o_ref[rr * GRID_W:(rr + 1) * GRID_W, :] = jnp.concatenate(outs, axis=1).astype(o_ref.dtype)


def _nat_bias_table(rpb):
    heads, n_row_off, n_col_off = rpb.shape
    cols = np.arange(GRID_W)
    col_start = np.clip(cols - NAT_KC // 2, 0, GRID_W - NAT_KC)
    inside = (cols[None, :] >= col_start[:, None]) & (cols[None, :] < col_start[:, None] + NAT_KC)
    period = 2 * GRID_W
    lead = GRID_W - NAT_KC
    sig = jnp.pad(rpb.astype(F32), ((0, 0), (0, 0), (lead, period - n_col_off - lead)))
    flat = jnp.tile(sig, (1, 1, GRID_W))[..., :GRID_W * (period - 1)]
    toep = flat.reshape(heads, n_row_off, GRID_W, period - 1)[..., GRID_W - 1:]
    toep = jnp.where(inside[None, None], toep, NEG_INF)
    bands = [jnp.transpose(toep[:, d:d + NAT_KR], (0, 2, 1, 3)).reshape(heads, GRID_W, NAT_KR * GRID_W)
             for d in range(NAT_KR)]
    return jnp.stack(bands, axis=0)


def _nat(qd, kd, vd, rpb, batch, seq):
    n = qd.shape[0]
    rows = seq // GRID_W
    assert rows >= NAT_KR
    bias = _nat_bias_table(rpb)
    rps = NAT_ROWS_PER_STEP
    assert rows % rps == 0
    steps = rows // rps
    once = pl.Buffered(1)
    return pl.pallas_call(
        functools.partial(_nat_kernel, rows=rows, rows_per_step=rps),
        grid=(batch, steps),
        in_specs=[pl.BlockSpec((rps * GRID_W, 512), lambda b, r: (b * steps + r, 0)),
                  pl.BlockSpec((seq, 512), lambda b, r: (b, 0), pipeline_mode=once),
                  pl.BlockSpec((seq, 512), lambda b, r: (b, 0), pipeline_mode=once),
                  pl.BlockSpec(bias.shape, lambda b, r: (0, 0, 0, 0), pipeline_mode=once)],
        out_specs=pl.BlockSpec((rps * GRID_W, 512), lambda b, r: (b * steps + r, 0)),
        out_shape=jax.ShapeDtypeStruct((n, 512), BF16),
        compiler_params=_params("parallel", "arbitrary"),
        name="nat",
    )(qd, kd, vd, bias)


def _merge_kernel(x_ref, oa_ref, ob_ref, oc_ref, od_ref, sh_ref, sc_ref, gm_ref, wg_ref, wb_ref, wo_ref,
                  g_ref, b_ref, o_ref, *, alpha):
    x = x_ref[...]
    ub = (_norm_rows(x) * (1.0 + sc_ref[...]) + sh_ref[...]).astype(BF16)
    merged = None
    for n, br_ref in enumerate((oa_ref, ob_ref, oc_ref, od_ref)):
        term = _sigmoid(_bdot(ub, wg_ref[n])) * _bdot(br_ref[...], wb_ref[n])
        merged = term if merged is None else merged + term
    y = _bdot(merged.astype(BF16), wo_ref[...])
    z = alpha * x + gm_ref[...] * y
    o_ref[...] = _norm_rows(z) * g_ref[...] + b_ref[...]


def _merge(x2, branches, shift, scale, gate, wg, wb, wo, ln_g, ln_b, alpha, seq, tm):
    n, d = x2.shape
    spt = seq // tm
    row = lambda i: (i, 0)
    mod = lambda i: (i // spt, 0, 0)
    once = pl.Buffered(1)
    return pl.pallas_call(
        functools.partial(_merge_kernel, alpha=alpha),
        grid=(n // tm,),
        in_specs=[pl.BlockSpec((tm, d), row)] + [pl.BlockSpec((tm, 512), row)] * 4
                 + [pl.BlockSpec((None, 1, d), mod)] * 3
                 + [pl.BlockSpec(wg.shape, lambda i: (0, 0, 0), pipeline_mode=once),
                    pl.BlockSpec(wb.shape, lambda i: (0, 0, 0), pipeline_mode=once),
                    pl.BlockSpec(wo.shape, lambda i: (0, 0), pipeline_mode=once),
                    pl.BlockSpec((1, d), lambda i: (0, 0)), pl.BlockSpec((1, d), lambda i: (0, 0))],
        out_specs=pl.BlockSpec((tm, d), row),
        out_shape=jax.ShapeDtypeStruct((n, d), F32),
        compiler_params=_params("parallel"),
        name="merge",
    )(x2, *branches, shift, scale, gate, wg, wb, wo, ln_g, ln_b)


def _peer_candidate_blocks(k):
    blocks = []
    for i in range(k):
        need = min(k, (k + 1) // (i + 1))
        if need == 1:
            assert (k - i) % 8 == 0
            blocks.append((i, 1))
            break
        blocks.append((i, -(-need // 8) * 8))
    return tuple(blocks)


_PEER_CAND_BLOCKS = _peer_candidate_blocks(PEER_TOPK)
_PEER_CAND_ROWS = sum((PEER_TOPK - i) if w == 1 else w for i, w in _PEER_CAND_BLOCKS)


def _take_top(w, count, on_max):
    for r in range(count):
        m = jnp.max(w, axis=0, keepdims=True)
        on_max(r, m)
        if r + 1 < count:
            w = jnp.where(w == m, -jnp.inf, w)


def _peer_kernel(x_ref, sh_ref, sc_ref, gf_ref, g_ref, b_ref, wqt_ref, keys_ref, u_ref, vt_ref, o_ref,
                 ub_scr, q_scr, s_scr, p1_scr, p2_scr, thr_scr, top_scr, cand_scr, w_scr, acc_scr, *,
                 alpha, ipc, sub):
    c = pl.program_id(1)
    nk = PEER_NKEYS
    k = PEER_TOPK

    @pl.when(c == 0)
    def _scores_and_thresholds():
        u = _norm_rows(x_ref[...]) * (1.0 + sc_ref[...]) + sh_ref[...]
        ub_scr[...] = jnp.transpose(u).astype(BF16)
        q_scr[...] = _bdot(wqt_ref[...], ub_scr[...]).astype(BF16)

        def head(h, carry):
            for p in range(2):
                idx = 2 * h + p
                rows = pl.multiple_of(idx * nk, nk)
                s = _bdot(keys_ref[idx], q_scr[pl.ds(rows, nk), :])
                s_scr[idx] = s

                def keep(r, m, p=p):
                    top_scr[p, r:r + 1, :] = m

                _take_top(s, k + 1, keep)
            a = top_scr[0, :k, :]
            b = top_scr[1, :k, :]
            row = 0
            for i, width in _PEER_CAND_BLOCKS:
                if width == 1:
                    cand_scr[row:row + k - i, :] = a[i:k, :] + b[0:1, :]
                    row += k - i
                else:
                    cand_scr[row:row + width, :] = a[i:i + 1, :] + b[0:width, :]
                    row += width
            best = a[0:1, :] + b[0:1, :]
            stats = {"z": jnp.zeros_like(best)}

            def tally(r, m):
                if r < k:
                    stats["z"] = stats["z"] + jnp.exp(m - best)
                if r == k - 1:
                    stats["last"] = m
                if r == k:
                    stats["next"] = m

            _take_top(cand_scr[...], k + 1, tally)
            runner_up = jnp.maximum(stats["next"], jnp.maximum(top_scr[0, k:k + 1, :] + b[0:1, :],
                                                               a[0:1, :] + top_scr[1, k:k + 1, :]))
            thr_scr[pl.ds(h, 1), :] = 0.5 * (stats["last"] + runner_up)
            p1_scr[h] = jnp.exp(s_scr[2 * h] - a[0:1, :])
            p2_scr[h] = jnp.exp(s_scr[2 * h + 1] - b[0:1, :]) / stats["z"]
            return carry

        lax.fori_loop(0, PEER_HEADS, head, 0)
        acc_scr[...] = jnp.zeros_like(acc_scr)

    ips = sub // nk
    n_sub = ipc // ips

    def hidden(sb):
        return _bdot(u_ref[sb * sub:(sb + 1) * sub, :], ub_scr[...])

    partial_out = None
    hid_next = hidden(0)
    for sb in range(n_sub):
        rows = slice(sb * sub, (sb + 1) * sub)
        hid = hid_next
        if sb + 1 < n_sub:
            hid_next = hidden(sb + 1)
        for il in range(ips):
            i = c * ipc + sb * ips + il
            gate = None
            for h in range(PEER_HEADS):
                s1_row = s_scr[2 * h, pl.ds(i, 1), :]
                p1_row = p1_scr[h, pl.ds(i, 1), :]
                chosen = s_scr[2 * h + 1] >= thr_scr[h:h + 1, :] - s1_row
                term = jnp.where(chosen, p2_scr[h], 0.0) * p1_row
                gate = term if gate is None else gate + term
            hb = hid[il * nk:(il + 1) * nk, :]
            act = 0.5 * hb * (1.0 + lax.erf(hb * (2.0 ** -0.5)))
            w_scr[sb * sub + il * nk:sb * sub + (il + 1) * nk, :] = (gate * act).astype(BF16)
        out = _bdot(vt_ref[:, rows], w_scr[rows, :])
        partial_out = out if partial_out is None else partial_out + out
    acc_scr[...] += partial_out

    @pl.when(c == pl.num_programs(1) - 1)
    def _finish():
        y = jnp.transpose(acc_scr[...])
        z = alpha * x_ref[...] + gf_ref[...] * y
        o_ref[...] = _norm_rows(z) * g_ref[...] + b_ref[...]


def _peer(x2, shift, scale, gate, ln_g, ln_b, wqt, keys, u_tab, vt_tab, alpha, seq, tt):
    n, d = x2.shape
    chunk, sub = PEER_CHUNK, PEER_SUB
    assert u_tab.shape[0] % chunk == 0 and chunk % sub == 0 and sub % PEER_NKEYS == 0
    spt = seq // tt
    experts = u_tab.shape[0]
    ipc = chunk // PEER_NKEYS
    nhp = 2 * PEER_HEADS
    row = lambda i, c: (i, 0)
    mod = lambda i, c: (i // spt, 0, 0)
    return pl.pallas_call(
        functools.partial(_peer_kernel, alpha=alpha, ipc=ipc, sub=sub),
        grid=(n // tt, experts // chunk),
        in_specs=[pl.BlockSpec((tt, d), row)] + [pl.BlockSpec((None, 1, d), mod)] * 3
                 + [pl.BlockSpec((1, d), lambda i, c: (0, 0)), pl.BlockSpec((1, d), lambda i, c: (0, 0)),
                    pl.BlockSpec(wqt.shape, lambda i, c: (0, 0), pipeline_mode=pl.Buffered(1)),
                    pl.BlockSpec(keys.shape, lambda i, c: (0, 0, 0), pipeline_mode=pl.Buffered(1)),
                    pl.BlockSpec((chunk, d), lambda i, c: (c, 0)),
                    pl.BlockSpec((d, chunk), lambda i, c: (0, c))],
        out_specs=pl.BlockSpec((tt, d), row),
        out_shape=jax.ShapeDtypeStruct((n, d), F32),
        scratch_shapes=[pltpu.VMEM((d, tt), BF16),
                        pltpu.VMEM((nhp * PEER_NKEYS, tt), BF16),
                        pltpu.VMEM((nhp, PEER_NKEYS, tt), F32),
                        pltpu.VMEM((PEER_HEADS, PEER_NKEYS, tt), F32),
                        pltpu.VMEM((PEER_HEADS, PEER_NKEYS, tt), F32),
                        pltpu.VMEM((PEER_HEADS, tt), F32),
                        pltpu.VMEM((2, PEER_TOPK + 8, tt), F32),
                        pltpu.VMEM((_PEER_CAND_ROWS, tt), F32),
                        pltpu.VMEM((chunk, tt), BF16),
                        pltpu.VMEM((d, tt), F32)],
        compiler_params=_params("parallel", "arbitrary"),
        name="peer",
    )(x2, shift, scale, gate, ln_g, ln_b, wqt, keys, u_tab, vt_tab)


def _rotate_half_cols(w, dim):
    rows, cols = w.shape
    w3 = w.reshape(rows, cols // dim, dim)
    return jnp.concatenate([-w3[..., dim // 2:], w3[..., :dim // 2]], axis=-1).reshape(rows, cols)


def _pad_cols(w, width):
    return jnp.pad(w, ((0, 0), (0, width - w.shape[1])))


def _in_proj_weights(w_in):
    widths = (512, 128, 128, 384, 256, 32, 512, 512, 512, 512, 512, 512)
    names = ("qa", "ka", "va", "cq", "ckv", "kr", "qc", "kc", "vc", "qd", "kd", "vd")
    parts, off = {}, 0
    for name, width in zip(names, widths):
        parts[name] = w_in[:, off:off + width]
        off += width
    for name, extra in (("qa", 1.0), ("qc", LOG2_E), ("qd", 1.0)):
        parts[name] = parts[name] * (HEAD_DIM ** -0.5 * extra)
    rot_dim = {"qa": HEAD_DIM, "ka": HEAD_DIM, "qc": DIFF_QK, "kc": DIFF_QK, "kr": MLA_ROPE}
    cols = [_pad_cols(parts[name], width) for name, width in _MAIN_GROUPS]
    cols += [_pad_cols(_rotate_half_cols(parts[name], rot_dim[name]), width) for name, width in _ROT_GROUPS]
    return jnp.concatenate(cols, axis=1).astype(BF16)


def _mla_weights(q_norm, q_up, kv_norm, kv_up):
    qr, kvr = q_up.shape[0], kv_up.shape[0]
    qh = q_up.reshape(qr, MLA_HEADS, MLA_NOPE + MLA_ROPE) * ((MLA_NOPE + MLA_ROPE) ** -0.5 * LOG2_E)
    pad = MLA_SLOT - MLA_NOPE - MLA_ROPE
    wq = jnp.pad(qh, ((0, 0), (0, 0), (0, pad))).reshape(qr, MLA_HEADS * MLA_SLOT)
    rot = _rotate_half_cols(qh[..., MLA_NOPE:].reshape(qr, MLA_HEADS * MLA_ROPE), MLA_ROPE)
    rot = rot.reshape(qr, MLA_HEADS, MLA_ROPE)
    wqr = jnp.pad(rot, ((0, 0), (0, 0), (MLA_NOPE, pad))).reshape(qr, MLA_HEADS * MLA_SLOT)
    kvh = kv_up.reshape(kvr, MLA_HEADS, MLA_NOPE + MLA_V)
    wk = jnp.pad(kvh[..., :MLA_NOPE], ((0, 0), (0, 0), (0, MLA_SLOT - MLA_NOPE))).reshape(kvr, MLA_HEADS * MLA_SLOT)
    wv = kvh[..., MLA_NOPE:].reshape(kvr, MLA_HEADS * MLA_V)
    place = np.zeros((128, MLA_HEADS, MLA_SLOT), np.float32)
    for r in range(MLA_ROPE):
        place[r, :, MLA_NOPE + r] = 1.0
    return {"qn": q_norm.reshape(1, qr), "kvn": kv_norm.reshape(1, kvr),
            "wq": wq.astype(BF16), "wqr": wqr.astype(BF16), "wk": wk.astype(BF16), "wv": wv.astype(BF16),
            "e": jnp.asarray(place.reshape(128, MLA_HEADS * MLA_SLOT), BF16)}


def _rope_tables(seq):
    def base(dim):
        inv = ROPE_THETA ** (-jnp.arange(0, dim, 2, dtype=F32) / dim)
        ang = jnp.arange(seq, dtype=F32)[:, None] * inv[None, :]
        cos, sin = jnp.cos(ang), jnp.sin(ang)
        return jnp.concatenate([cos, cos], axis=1), jnp.concatenate([sin, sin], axis=1)

    cos64, sin64 = base(HEAD_DIM)
    cos32, sin32 = base(MLA_ROPE)
    ones = jnp.ones((seq, MLA_NOPE), F32)
    zeros = jnp.zeros((seq, MLA_NOPE), F32)
    tail = jnp.zeros((seq, MLA_SLOT - MLA_NOPE - MLA_ROPE), F32)
    return {"cos64": jnp.concatenate([cos64, cos64], axis=1), "sin64": jnp.concatenate([sin64, sin64], axis=1),
            "cos32": _pad_cols(cos32, 128), "sin32": _pad_cols(sin32, 128),
            "mla_cos": jnp.concatenate([ones, cos32, tail], axis=1),
            "mla_sin": jnp.concatenate([zeros, sin32, tail], axis=1)}


def _tile(total, want):
    t = min(total, want)
    assert total % t == 0
    return t


def kernel(x, c, ada_w, ada_b, w_in, swa_sink, mla_q_norm, mla_q_up, mla_kv_norm, mla_kv_up, diff_lambda_q1, diff_lambda_k1, diff_lambda_q2, diff_lambda_k2, diff_subln, nat_rpb, w_gate, w_branch, w_out, ln1_g, ln1_b, peer_wq, peer_keys, peer_u, peer_v, ln2_g, ln2_b):
    batch, seq, d = x.shape
    depth = ada_w.shape[0]
    alpha = (2 * depth) ** 0.25
    n = batch * seq
    tabs = _rope_tables(seq)
    mod = _ada(c, ada_w, ada_b)
    tm = _tile(seq, 512)
    x2 = x.reshape(n, d)
    for l in range(depth):
        sh_mix, sc_mix, g_mix, sh_ffn, sc_ffn, g_ffn = (mod[l, :, j] for j in range(6))
        (qa, ka, va, cq, ckv, kr, qc, kc, vc, qd, kd, vd) = _in_proj(
            x2, sh_mix, sc_mix, _in_proj_weights(w_in[l]), tabs, seq, tm)
        o_a = _swa(qa, ka, va, swa_sink[l], batch, seq, _tile(seq, 256))
        qm, km, vm = _mla_prep(cq, ckv, kr, _mla_weights(mla_q_norm[l], mla_q_up[l], mla_kv_norm[l], mla_kv_up[l]),
                               tabs, seq, tm)
        o_b = _mla(qm, km, vm, batch, seq, _tile(seq, FLASH_TQ), _tile(seq, FLASH_TK))
        lambda_init = 0.8 - 0.6 * math.exp(-0.3 * l)
        lams = tuple(v[l].reshape(1, DIFF_QK) for v in (diff_lambda_q1, diff_lambda_k1, diff_lambda_q2, diff_lambda_k2))
        o_c = _diff(qc, kc, vc, lams, diff_subln[l].reshape(1, DIFF_V), lambda_init, batch, seq,
                    _tile(seq, FLASH_TQ), _tile(seq, FLASH_TK))
        o_d = _nat(qd, kd, vd, nat_rpb[l], batch, seq)
        x2 = _merge(x2, (o_a, o_b, o_c, o_d), sh_mix, sc_mix, g_mix, w_gate[l].astype(BF16),
                    w_branch[l].astype(BF16), w_out[l].astype(BF16), ln1_g[l].reshape(1, d), ln1_b[l].reshape(1, d),
                    alpha, seq, tm)
        keys = peer_keys[l].reshape(2 * PEER_HEADS, PEER_NKEYS, PEER_DKEY // 2).astype(BF16)
        x2 = _peer(x2, sh_ffn, sc_ffn, g_ffn, ln2_g[l].reshape(1, d), ln2_b[l].reshape(1, d),
                   peer_wq[l].T.astype(BF16), keys, peer_u[l].astype(BF16), peer_v[l].T.astype(BF16),
                   alpha, seq, tm)
    return x2.reshape(batch, seq, d)
```

```python
import functools
import math

import jax
import jax.numpy as jnp
import numpy as np
from jax import lax
from jax.experimental import pallas as pl
from jax.experimental.pallas import tpu as pltpu

F32 = jnp.float32
BF16 = jnp.bfloat16

GRID_W = 64
ROPE_THETA = 10000.0
HEAD_DIM = 64
LN_EPS = 1e-5
NEG_INF = -1e30
LOG2_E = 1.4426950408889634

SWA_HEADS = 8
SWA_KV_HEADS = 2
SWA_WINDOW = 128
MLA_HEADS = 8
MLA_NOPE = 64
MLA_ROPE = 32
MLA_V = 64
MLA_SLOT = 128
DIFF_HEADS = 4
DIFF_QK = 64
DIFF_V = 128
NAT_HEADS = 8
NAT_KR = 8
NAT_KC = 16
PEER_HEADS = 8
PEER_NKEYS = 128
PEER_DKEY = 256
PEER_TOPK = 16

VMEM_LIMIT_BYTES = 56 * 1024 * 1024
FLASH_TQ = 2048
FLASH_Q_BLOCKS = 4
FLASH_TK = 1024
NAT_ROWS_PER_STEP = 4
PEER_CHUNK = 2048
PEER_SUB = 512

_MAIN_GROUPS = (("qa", 512), ("ka", 128), ("va", 128), ("cq", 384), ("ckv", 256), ("qc", 512), ("kc", 512),
                ("vc", 512), ("qd", 512), ("kd", 512), ("vd", 512), ("kr", 128))
_ROT_GROUPS = (("qa", 512), ("ka", 128), ("qc", 512), ("kc", 512), ("kr", 128))


def _offsets(groups, base=0):
    out, off = {}, base
    for name, width in groups:
        out[name] = (off, off + width)
        off += width
    return out, off


_MAIN_OFF, _MAIN_END = _offsets(_MAIN_GROUPS)
_ROT_OFF, _W_ALL_COLS = _offsets(_ROT_GROUPS, _MAIN_END)

_NT = (((1,), (1,)), ((), ()))


def _params(*sem):
    return pltpu.CompilerParams(dimension_semantics=sem, vmem_limit_bytes=VMEM_LIMIT_BYTES)


def _norm_rows(x):
    mu = jnp.mean(x, axis=-1, keepdims=True)
    xc = x - mu
    var = jnp.mean(xc * xc, axis=-1, keepdims=True)
    return xc * lax.rsqrt(var + LN_EPS)


def _sigmoid(z):
    return 1.0 / (1.0 + jnp.exp(-z))


def _bdot(a, b):
    return jnp.dot(a, b, preferred_element_type=F32)


def _ada_kernel(c_ref, w_ref, b_ref, o_ref):
    c = c_ref[...]
    act = c * _sigmoid(c)
    o_ref[...] = jnp.dot(act, w_ref[...], preferred_element_type=F32,
                         precision=lax.Precision.HIGHEST) + b_ref[...]


def _ada(c, ada_w, ada_b):
    depth, d, six_d = ada_w.shape
    b = c.shape[0]
    rows = -(-b // 8) * 8
    c_pad = jnp.pad(c, ((0, rows - b), (0, 0)))
    out = pl.pallas_call(
        _ada_kernel,
        grid=(depth, six_d // d),
        in_specs=[pl.BlockSpec((rows, d), lambda l, j: (0, 0)),
                  pl.BlockSpec((None, d, d), lambda l, j: (l, 0, j)),
                  pl.BlockSpec((None, 1, d), lambda l, j: (l, 0, j))],
        out_specs=pl.BlockSpec((None, rows, d), lambda l, j: (l, 0, j)),
        out_shape=jax.ShapeDtypeStruct((depth, rows, six_d), F32),
        compiler_params=_params("parallel", "parallel"),
        name="ada",
    )(c_pad, ada_w, ada_b.reshape(depth, 1, six_d))
    return out[:, :b].reshape(depth, b, 6, 1, d)


def _in_proj_kernel(x_ref, sh_ref, sc_ref, w_ref, cos_ref, sin_ref, cos32_ref, sin32_ref,
                    qa_ref, ka_ref, va_ref, cq_ref, ckv_ref, kr_ref, qc_ref, kc_ref, vc_ref,
                    qd_ref, kd_ref, vd_ref):
    u = _norm_rows(x_ref[...]) * (1.0 + sc_ref[...]) + sh_ref[...]
    ub = u.astype(BF16)

    def proj(off):
        return _bdot(ub, w_ref[:, off[0]:off[1]])

    def roped(name, cos, sin):
        main, rot = proj(_MAIN_OFF[name]), proj(_ROT_OFF[name])
        reps = main.shape[1] // cos.shape[1]
        if reps > 1:
            cos = jnp.concatenate([cos] * reps, axis=1)
            sin = jnp.concatenate([sin] * reps, axis=1)
        return main * cos + rot * sin

    cos, sin = cos_ref[...], sin_ref[...]
    qa_ref[...] = roped("qa", cos, sin).astype(qa_ref.dtype)
    ka_ref[...] = roped("ka", cos, sin).astype(ka_ref.dtype)
    va_ref[...] = proj(_MAIN_OFF["va"]).astype(va_ref.dtype)
    cq_ref[...] = proj(_MAIN_OFF["cq"])
    ckv_ref[...] = proj(_MAIN_OFF["ckv"])
    kr_ref[...] = roped("kr", cos32_ref[...], sin32_ref[...]).astype(kr_ref.dtype)
    qc_ref[...] = roped("qc", cos, sin).astype(qc_ref.dtype)
    kc_ref[...] = roped("kc", cos, sin).astype(kc_ref.dtype)
    vc_ref[...] = proj(_MAIN_OFF["vc"]).astype(vc_ref.dtype)
    qd_ref[...] = proj(_MAIN_OFF["qd"]).astype(qd_ref.dtype)
    kd_ref[...] = proj(_MAIN_OFF["kd"]).astype(kd_ref.dtype)
    vd_ref[...] = proj(_MAIN_OFF["vd"]).astype(vd_ref.dtype)


def _in_proj(x2, shift, scale, w_all, tabs, seq, tm):
    n, d = x2.shape
    spt = seq // tm
    widths = dict(_MAIN_GROUPS)
    names = ("qa", "ka", "va", "cq", "ckv", "kr", "qc", "kc", "vc", "qd", "kd", "vd")
    dtypes = {k: BF16 for k in names}
    dtypes["cq"] = F32
    dtypes["ckv"] = F32
    row = lambda i: (i, 0)
    mod = lambda i: (i // spt, 0, 0)
    tab = lambda i: (i % spt, 0)
    return pl.pallas_call(
        _in_proj_kernel,
        grid=(n // tm,),
        in_specs=[pl.BlockSpec((tm, d), row),
                  pl.BlockSpec((None, 1, d), mod),
                  pl.BlockSpec((None, 1, d), mod),
                  pl.BlockSpec((d, _W_ALL_COLS), lambda i: (0, 0), pipeline_mode=pl.Buffered(1)),
                  pl.BlockSpec((tm, 128), tab), pl.BlockSpec((tm, 128), tab),
                  pl.BlockSpec((tm, 128), tab), pl.BlockSpec((tm, 128), tab)],
        out_specs=[pl.BlockSpec((tm, widths[k]), row) for k in names],
        out_shape=[jax.ShapeDtypeStruct((n, widths[k]), dtypes[k]) for k in names],
        compiler_params=_params("parallel"),
        name="in_proj",
    )(x2, shift, scale, w_all, tabs["cos64"], tabs["sin64"], tabs["cos32"], tabs["sin32"])


def _swa_kernel(q_ref, k_ref, v_ref, sink_ref, o_ref, *, tq, seq):
    i = pl.program_id(1)
    span = tq + 2 * SWA_WINDOW
    start = pl.multiple_of(jnp.clip(i * tq - SWA_WINDOW, 0, seq - span), SWA_WINDOW)
    ks = k_ref[pl.ds(start, span), :]
    vs = v_ref[pl.ds(start, span), :]
    qpos = i * tq + lax.broadcasted_iota(jnp.int32, (tq, span), 0)
    kpos = start + lax.broadcasted_iota(jnp.int32, (tq, span), 1)
    valid = jnp.abs(qpos - kpos) <= SWA_WINDOW
    q = q_ref[...]
    grp = SWA_HEADS // SWA_KV_HEADS
    outs = []
    for h in range(SWA_HEADS):
        kv = h // grp
        qh = q[:, h * HEAD_DIM:(h + 1) * HEAD_DIM]
        kh = ks[:, kv * HEAD_DIM:(kv + 1) * HEAD_DIM]
        vh = vs[:, kv * HEAD_DIM:(kv + 1) * HEAD_DIM]
        s = lax.dot_general(qh, kh, _NT, preferred_element_type=F32)
        s = jnp.where(valid, s, NEG_INF)
        sink = sink_ref[h][:, :1]
        m = jnp.maximum(jnp.max(s, axis=-1, keepdims=True), sink)
        p = jnp.exp(s - m)
        denom = jnp.sum(p, axis=-1, keepdims=True) + jnp.exp(sink - m)
        outs.append(_bdot(p.astype(BF16), vh) / denom)
    o_ref[...] = jnp.concatenate(outs, axis=1).astype(o_ref.dtype)


def _swa(qa, ka, va, sink, batch, seq, tq):
    n = qa.shape[0]
    nq = seq // tq
    sink_b = jnp.broadcast_to(sink.astype(F32)[:, None, None], (SWA_HEADS, 1, 128))
    return pl.pallas_call(
        functools.partial(_swa_kernel, tq=tq, seq=seq),
        grid=(batch, nq),
        in_specs=[pl.BlockSpec((tq, 512), lambda b, i: (b * nq + i, 0)),
                  pl.BlockSpec((seq, 128), lambda b, i: (b, 0)),
                  pl.BlockSpec((seq, 128), lambda b, i: (b, 0)),
                  pl.BlockSpec((SWA_HEADS, 1, 128), lambda b, i: (0, 0, 0))],
        out_specs=pl.BlockSpec((tq, 512), lambda b, i: (b * nq + i, 0)),
        out_shape=jax.ShapeDtypeStruct((n, 512), BF16),
        compiler_params=_params("parallel", "parallel"),
        name="swa",
    )(qa, ka, va, sink_b)


def _mla_prep_kernel(cq_ref, ckv_ref, kr_ref, qn_ref, kvn_ref, wq_ref, wqr_ref, wk_ref, e_ref, wv_ref,
                     ct_ref, st_ref, qm_ref, km_ref, vm_ref):
    def rms(x, g):
        return (x * lax.rsqrt(jnp.mean(x * x, axis=-1, keepdims=True) + LN_EPS) * g).astype(BF16)

    qn = rms(cq_ref[...], qn_ref[...])
    kvn = rms(ckv_ref[...], kvn_ref[...])
    cos = jnp.concatenate([ct_ref[...]] * MLA_HEADS, axis=1)
    sin = jnp.concatenate([st_ref[...]] * MLA_HEADS, axis=1)
    qm_ref[...] = (_bdot(qn, wq_ref[...]) * cos + _bdot(qn, wqr_ref[...]) * sin).astype(qm_ref.dtype)
    km_ref[...] = (_bdot(kvn, wk_ref[...]) + _bdot(kr_ref[...], e_ref[...])).astype(km_ref.dtype)
    vm_ref[...] = _bdot(kvn, wv_ref[...]).astype(vm_ref.dtype)


def _mla_prep(cq, ckv, kr, w, tabs, seq, tm):
    n = cq.shape[0]
    spt = seq // tm
    row = lambda i: (i, 0)
    full = lambda i: (0, 0)
    tab = lambda i: (i % spt, 0)
    hw = MLA_HEADS * MLA_SLOT
    return pl.pallas_call(
        _mla_prep_kernel,
        grid=(n // tm,),
        in_specs=[pl.BlockSpec((tm, cq.shape[1]), row), pl.BlockSpec((tm, ckv.shape[1]), row),
                  pl.BlockSpec((tm, 128), row),
                  pl.BlockSpec((1, cq.shape[1]), full), pl.BlockSpec((1, ckv.shape[1]), full),
                  pl.BlockSpec(w["wq"].shape, full), pl.BlockSpec(w["wqr"].shape, full),
                  pl.BlockSpec(w["wk"].shape, full), pl.BlockSpec(w["e"].shape, full),
                  pl.BlockSpec(w["wv"].shape, full),
                  pl.BlockSpec((tm, 128), tab), pl.BlockSpec((tm, 128), tab)],
        out_specs=[pl.BlockSpec((tm, hw), row), pl.BlockSpec((tm, hw), row),
                   pl.BlockSpec((tm, MLA_HEADS * MLA_V), row)],
        out_shape=[jax.ShapeDtypeStruct((n, hw), BF16), jax.ShapeDtypeStruct((n, hw), BF16),
                   jax.ShapeDtypeStruct((n, MLA_HEADS * MLA_V), BF16)],
        compiler_params=_params("parallel"),
        name="mla_prep",
    )(cq, ckv, kr, w["qn"], w["kvn"], w["wq"], w["wqr"], w["wk"], w["e"], w["wv"],
      tabs["mla_cos"], tabs["mla_sin"])


def _flash_chains(qs, k_ref, vt_ref, k_lanes, seq, tk):
    dv = vt_ref.shape[1]

    def step(kt, carry):
        off = pl.multiple_of(kt * tk, tk)
        vt = vt_ref[kt]
        out = []
        scores = [lax.dot_general(k_ref[pl.ds(off, tk), lanes[0]:lanes[1]], q, _NT, preferred_element_type=F32)
                  for q, lanes in zip(qs, k_lanes)]
        for s, (m, l, acc) in zip(scores, carry):
            m_new = jnp.maximum(m, jnp.max(s, axis=0, keepdims=True))
            alpha = jnp.exp2(m - m_new)
            p = jnp.exp2(s - m_new)
            l = alpha * l + jnp.sum(p, axis=0, keepdims=True)
            acc = alpha * acc + _bdot(vt, p.astype(BF16))
            out.append((m_new, l, acc))
        return tuple(out)

    init = tuple((jnp.full((1, q.shape[0]), NEG_INF, F32), jnp.zeros((1, q.shape[0]), F32),
                  jnp.zeros((dv, q.shape[0]), F32)) for q in qs)
    final = lax.fori_loop(0, seq // tk, step, init)
    return [(acc, l) for _, l, acc in final]


def _values_transposed(v, batch, seq, tk):
    groups = v.shape[1] // 128
    v5 = v.reshape(batch, seq // tk, tk, groups, 128)
    return jnp.transpose(v5, (0, 3, 1, 4, 2))


def _mla_kernel(q_ref, k_ref, vt_ref, o_ref, *, seq, tk):
    rows = q_ref.shape[0] // FLASH_Q_BLOCKS
    lanes = [(hh * MLA_SLOT, (hh + 1) * MLA_SLOT) for hh in range(2)]
    qs = [q_ref[r * rows:(r + 1) * rows, lo:hi] for r in range(FLASH_Q_BLOCKS) for lo, hi in lanes]
    outs = [acc / l for acc, l in _flash_chains(qs, k_ref, vt_ref, lanes * FLASH_Q_BLOCKS, seq, tk)]
    row = lax.broadcasted_iota(jnp.int32, outs[0].shape, 0)
    for r in range(FLASH_Q_BLOCKS):
        o_t = jnp.where(row < MLA_V, outs[2 * r], outs[2 * r + 1])
        o_ref[r * rows:(r + 1) * rows, :] = jnp.transpose(o_t).astype(o_ref.dtype)


def _mla(qm, km, vm, batch, seq, tq, tk):
    n = qm.shape[0]
    nq = seq // tq
    pairs = MLA_HEADS // 2
    return pl.pallas_call(
        functools.partial(_mla_kernel, seq=seq, tk=tk),
        grid=(batch, pairs, nq),
        in_specs=[pl.BlockSpec((tq, 2 * MLA_SLOT), lambda b, h, i: (b * nq + i, h)),
                  pl.BlockSpec((seq, 2 * MLA_SLOT), lambda b, h, i: (b, h)),
                  pl.BlockSpec((None, None, seq // tk, 2 * MLA_V, tk), lambda b, h, i: (b, h, 0, 0, 0))],
        out_specs=pl.BlockSpec((tq, 2 * MLA_V), lambda b, h, i: (b * nq + i, h)),
        out_shape=jax.ShapeDtypeStruct((n, MLA_HEADS * MLA_V), BF16),
        compiler_params=_params("parallel", "parallel", "parallel"),
        name="mla",
    )(qm, km, _values_transposed(vm, batch, seq, tk))


def _diff_kernel(q_ref, k_ref, vt_ref, lq1_ref, lk1_ref, lq2_ref, lk2_ref, sub_ref, o_ref, *,
                 seq, tk, lambda_init):
    rows = q_ref.shape[0] // FLASH_Q_BLOCKS
    lane = lax.broadcasted_iota(jnp.int32, (rows, 2 * DIFF_QK), 1)
    qs = []
    for r in range(FLASH_Q_BLOCKS):
        q = q_ref[r * rows:(r + 1) * rows, :]
        zero = jnp.zeros_like(q)
        qs += [jnp.where(lane < DIFF_QK, q, zero), jnp.where(lane >= DIFF_QK, q, zero)]
    res = _flash_chains(qs, k_ref, vt_ref, [(0, 2 * DIFF_QK)] * len(qs), seq, tk)
    lam = (jnp.exp(jnp.sum(lq1_ref[...] * lk1_ref[...], axis=-1, keepdims=True))
           - jnp.exp(jnp.sum(lq2_ref[...] * lk2_ref[...], axis=-1, keepdims=True)) + lambda_init)
    for r in range(FLASH_Q_BLOCKS):
        (acc1, l1), (acc2, l2) = res[2 * r], res[2 * r + 1]
        o = jnp.transpose(acc1 / l1 - lam * (acc2 / l2))
        o = o * lax.rsqrt(jnp.mean(o * o, axis=-1, keepdims=True) + LN_EPS) * sub_ref[...]
        o_ref[r * rows:(r + 1) * rows, :] = (o * (1.0 - lambda_init)).astype(o_ref.dtype)


def _diff(qc, kc, vc, lams, subln, lambda_init, batch, seq, tq, tk):
    n = qc.shape[0]
    nq = seq // tq
    vec = lambda b, h, i: (0, 0)
    return pl.pallas_call(
        functools.partial(_diff_kernel, seq=seq, tk=tk, lambda_init=lambda_init),
        grid=(batch, DIFF_HEADS, nq),
        in_specs=[pl.BlockSpec((tq, 2 * DIFF_QK), lambda b, h, i: (b * nq + i, h)),
                  pl.BlockSpec((seq, 2 * DIFF_QK), lambda b, h, i: (b, h)),
                  pl.BlockSpec((None, None, seq // tk, DIFF_V, tk), lambda b, h, i: (b, h, 0, 0, 0)),
                  pl.BlockSpec((1, DIFF_QK), vec), pl.BlockSpec((1, DIFF_QK), vec),
                  pl.BlockSpec((1, DIFF_QK), vec), pl.BlockSpec((1, DIFF_QK), vec),
                  pl.BlockSpec((1, DIFF_V), vec)],
        out_specs=pl.BlockSpec((tq, DIFF_V), lambda b, h, i: (b * nq + i, h)),
        out_shape=jax.ShapeDtypeStruct((n, DIFF_HEADS * DIFF_V), BF16),
        compiler_params=_params("parallel", "parallel", "parallel"),
        name="diff",
    )(qc, kc, _values_transposed(vc, batch, seq, tk), *lams, subln)


def _nat_row_start(r, rows):
    return jnp.clip(r - NAT_KR // 2, 0, rows - NAT_KR)


def _nat_kernel(q_ref, k_ref, v_ref, bias_ref, o_ref, *, rows, rows_per_step):
    heads = [slice(h * HEAD_DIM, (h + 1) * HEAD_DIM) for h in range(NAT_HEADS)]
    values, logits = [], []
    for rr in range(rows_per_step):
        r = pl.program_id(1) * rows_per_step + rr
        r0 = _nat_row_start(r, rows)
        band = r0 - r + (NAT_KR - 1)
        off = pl.multiple_of(r0 * GRID_W, GRID_W)
        ks = k_ref[pl.ds(off, NAT_KR * GRID_W), :]
        values.append(v_ref[pl.ds(off, NAT_KR * GRID_W), :])
        q = q_ref[rr * GRID_W:(rr + 1) * GRID_W, :]
        logits.append([lax.dot_general(q[:, sl], ks[:, sl], _NT, preferred_element_type=F32) + bias_ref[band, h]
                       for h, sl in enumerate(heads)])
    probs = []
    for row_logits in logits:
        row_probs = []
        for s in row_logits:
            p = jnp.exp(s - jnp.max(s, axis=-1, keepdims=True))
            row_probs.append((p.astype(BF16), jnp.sum(p, axis=-1, keepdims=True)))
        probs.append(row_probs)
    for rr, (row_probs, vs) in enumerate(zip(probs, values)):
        outs = [_bdot(p, vs[:, sl]) / denom for (p, denom), sl in zip(row_probs, heads)]
        o_ref[rr * GRID_W:(rr + 1) * GRID_W, :] = jnp.concatenate(outs, axis=1).astype(o_ref.dtype)


def _nat_bias_table(rpb):
    heads, n_row_off, n_col_off = rpb.shape
    cols = np.arange(GRID_W)
    col_start = np.clip(cols - NAT_KC // 2, 0, GRID_W - NAT_KC)
    inside = (cols[None, :] >= col_start[:, None]) & (cols[None, :] < col_start[:, None] + NAT_KC)
    period = 2 * GRID_W
    lead = GRID_W - NAT_KC
    sig = jnp.pad(rpb.astype(F32), ((0, 0), (0, 0), (lead, period - n_col_off - lead)))
    flat = jnp.tile(sig, (1, 1, GRID_W))[..., :GRID_W * (period - 1)]
    toep = flat.reshape(heads, n_row_off, GRID_W, period - 1)[..., GRID_W - 1:]
    toep = jnp.where(inside[None, None], toep, NEG_INF)
    bands = [jnp.transpose(toep[:, d:d + NAT_KR], (0, 2, 1, 3)).reshape(heads, GRID_W, NAT_KR * GRID_W)
             for d in range(NAT_KR)]
    return jnp.stack(bands, axis=0)


def _nat(qd, kd, vd, rpb, batch, seq):
    n = qd.shape[0]
    rows = seq // GRID_W
    assert rows >= NAT_KR
    bias = _nat_bias_table(rpb)
    rps = NAT_ROWS_PER_STEP
    assert rows % rps == 0
    steps = rows // rps
    once = pl.Buffered(1)
    return pl.pallas_call(
        functools.partial(_nat_kernel, rows=rows, rows_per_step=rps),
        grid=(batch, steps),
        in_specs=[pl.BlockSpec((rps * GRID_W, 512), lambda b, r: (b * steps + r, 0)),
                  pl.BlockSpec((seq, 512), lambda b, r: (b, 0), pipeline_mode=once),
                  pl.BlockSpec((seq, 512), lambda b, r: (b, 0), pipeline_mode=once),
                  pl.BlockSpec(bias.shape, lambda b, r: (0, 0, 0, 0), pipeline_mode=once)],
        out_specs=pl.BlockSpec((rps * GRID_W, 512), lambda b, r: (b * steps + r, 0)),
        out_shape=jax.ShapeDtypeStruct((n, 512), BF16),
        compiler_params=_params("parallel", "arbitrary"),
        name="nat",
    )(qd, kd, vd, bias)


def _merge_kernel(x_ref, oa_ref, ob_ref, oc_ref, od_ref, sh_ref, sc_ref, gm_ref, wg_ref, wb_ref, wo_ref,
                  g_ref, b_ref, o_ref, *, alpha):
    x = x_ref[...]
    ub = (_norm_rows(x) * (1.0 + sc_ref[...]) + sh_ref[...]).astype(BF16)
    merged = None
    for n, br_ref in enumerate((oa_ref, ob_ref, oc_ref, od_ref)):
        term = _sigmoid(_bdot(ub, wg_ref[n])) * _bdot(br_ref[...], wb_ref[n])
        merged = term if merged is None else merged + term
    y = _bdot(merged.astype(BF16), wo_ref[...])
    z = alpha * x + gm_ref[...] * y
    o_ref[...] = _norm_rows(z) * g_ref[...] + b_ref[...]


def _merge(x2, branches, shift, scale, gate, wg, wb, wo, ln_g, ln_b, alpha, seq, tm):
    n, d = x2.shape
    spt = seq // tm
    row = lambda i: (i, 0)
    mod = lambda i: (i // spt, 0, 0)
    once = pl.Buffered(1)
    return pl.pallas_call(
        functools.partial(_merge_kernel, alpha=alpha),
        grid=(n // tm,),
        in_specs=[pl.BlockSpec((tm, d), row)] + [pl.BlockSpec((tm, 512), row)] * 4
                 + [pl.BlockSpec((None, 1, d), mod)] * 3
                 + [pl.BlockSpec(wg.shape, lambda i: (0, 0, 0), pipeline_mode=once),
                    pl.BlockSpec(wb.shape, lambda i: (0, 0, 0), pipeline_mode=once),
                    pl.BlockSpec(wo.shape, lambda i: (0, 0), pipeline_mode=once),
                    pl.BlockSpec((1, d), lambda i: (0, 0)), pl.BlockSpec((1, d), lambda i: (0, 0))],
        out_specs=pl.BlockSpec((tm, d), row),
        out_shape=jax.ShapeDtypeStruct((n, d), F32),
        compiler_params=_params("parallel"),
        name="merge",
    )(x2, *branches, shift, scale, gate, wg, wb, wo, ln_g, ln_b)


def _peer_candidate_blocks(k):
    blocks = []
    for i in range(k):
        need = min(k, (k + 1) // (i + 1))
        if need == 1:
            assert (k - i) % 8 == 0
            blocks.append((i, 1))
            break
        blocks.append((i, -(-need // 8) * 8))
    return tuple(blocks)


_PEER_CAND_BLOCKS = _peer_candidate_blocks(PEER_TOPK)
_PEER_CAND_ROWS = sum((PEER_TOPK - i) if w == 1 else w for i, w in _PEER_CAND_BLOCKS)


def _take_top(w, count, on_max):
    for r in range(count):
        m = jnp.max(w, axis=0, keepdims=True)
        on_max(r, m)
        if r + 1 < count:
            w = jnp.where(w == m, -jnp.inf, w)


def _peer_kernel(x_ref, sh_ref, sc_ref, gf_ref, g_ref, b_ref, wqt_ref, keys_ref, u_ref, vt_ref, o_ref,
                 ub_scr, q_scr, s_scr, p1_scr, p2_scr, thr_scr, top_scr, cand_scr, w_scr, acc_scr, *,
                 alpha, ipc, sub):
    c = pl.program_id(1)
    nk = PEER_NKEYS
    k = PEER_TOPK

    @pl.when(c == 0)
    def _scores_and_thresholds():
        u = _norm_rows(x_ref[...]) * (1.0 + sc_ref[...]) + sh_ref[...]
        ub_scr[...] = jnp.transpose(u).astype(BF16)
        q_scr[...] = _bdot(wqt_ref[...], ub_scr[...]).astype(BF16)

        def head(h, carry):
            for p in range(2):
                idx = 2 * h + p
                rows = pl.multiple_of(idx * nk, nk)
                s = _bdot(keys_ref[idx], q_scr[pl.ds(rows, nk), :])
                s_scr[idx] = s

                def keep(r, m, p=p):
                    top_scr[p, r:r + 1, :] = m

                _take_top(s, k + 1, keep)
            a = top_scr[0, :k, :]
            b = top_scr[1, :k, :]
            row = 0
            for i, width in _PEER_CAND_BLOCKS:
                if width == 1:
                    cand_scr[row:row + k - i, :] = a[i:k, :] + b[0:1, :]
                    row += k - i
                else:
                    cand_scr[row:row + width, :] = a[i:i + 1, :] + b[0:width, :]
                    row += width
            best = a[0:1, :] + b[0:1, :]
            stats = {"z": jnp.zeros_like(best)}

            def tally(r, m):
                if r < k:
                    stats["z"] = stats["z"] + jnp.exp(m - best)
                if r == k - 1:
                    stats["last"] = m
                if r == k:
                    stats["next"] = m

            _take_top(cand_scr[...], k + 1, tally)
            runner_up = jnp.maximum(stats["next"], jnp.maximum(top_scr[0, k:k + 1, :] + b[0:1, :],
                                                               a[0:1, :] + top_scr[1, k:k + 1, :]))
            thr_scr[pl.ds(h, 1), :] = 0.5 * (stats["last"] + runner_up)
            p1_scr[h] = jnp.exp(s_scr[2 * h] - a[0:1, :])
            p2_scr[h] = jnp.exp(s_scr[2 * h + 1] - b[0:1, :]) / stats["z"]
            return carry

        lax.fori_loop(0, PEER_HEADS, head, 0)
        acc_scr[...] = jnp.zeros_like(acc_scr)

    ips = sub // nk
    n_sub = ipc // ips

    def hidden(sb):
        return _bdot(u_ref[sb * sub:(sb + 1) * sub, :], ub_scr[...])

    partial_out = None
    hids = [hidden(sb) for sb in range(n_sub)]
    for sb in range(n_sub):
        rows = slice(sb * sub, (sb + 1) * sub)
        hid = hids[sb]
        for il in range(ips):
            i = c * ipc + sb * ips + il
            gate = None
            for h in range(PEER_HEADS):
                s1_row = s_scr[2 * h, pl.ds(i, 1), :]
                p1_row = p1_scr[h, pl.ds(i, 1), :]
                chosen = s_scr[2 * h + 1] >= thr_scr[h:h + 1, :] - s1_row
                term = jnp.where(chosen, p2_scr[h], 0.0) * p1_row
                gate = term if gate is None else gate + term
            hb = hid[il * nk:(il + 1) * nk, :]
            act = 0.5 * hb * (1.0 + lax.erf(hb * (2.0 ** -0.5)))
            w_scr[sb * sub + il * nk:sb * sub + (il + 1) * nk, :] = (gate * act).astype(BF16)
        out = _bdot(vt_ref[:, rows], w_scr[rows, :])
        partial_out = out if partial_out is None else partial_out + out
    acc_scr[...] += partial_out

    @pl.when(c == pl.num_programs(1) - 1)
    def _finish():
        y = jnp.transpose(acc_scr[...])
        z = alpha * x_ref[...] + gf_ref[...] * y
        o_ref[...] = _norm_rows(z) * g_ref[...] + b_ref[...]


def _peer(x2, shift, scale, gate, ln_g, ln_b, wqt, keys, u_tab, vt_tab, alpha, seq, tt):
    n, d = x2.shape
    chunk, sub = PEER_CHUNK, PEER_SUB
    assert u_tab.shape[0] % chunk == 0 and chunk % sub == 0 and sub % PEER_NKEYS == 0
    spt = seq // tt
    experts = u_tab.shape[0]
    ipc = chunk // PEER_NKEYS
    nhp = 2 * PEER_HEADS
    row = lambda i, c: (i, 0)
    mod = lambda i, c: (i // spt, 0, 0)
    return pl.pallas_call(
        functools.partial(_peer_kernel, alpha=alpha, ipc=ipc, sub=sub),
        grid=(n // tt, experts // chunk),
        in_specs=[pl.BlockSpec((tt, d), row)] + [pl.BlockSpec((None, 1, d), mod)] * 3
                 + [pl.BlockSpec((1, d), lambda i, c: (0, 0)), pl.BlockSpec((1, d), lambda i, c: (0, 0)),
                    pl.BlockSpec(wqt.shape, lambda i, c: (0, 0), pipeline_mode=pl.Buffered(1)),
                    pl.BlockSpec(keys.shape, lambda i, c: (0, 0, 0), pipeline_mode=pl.Buffered(1)),
                    pl.BlockSpec((chunk, d), lambda i, c: (c, 0)),
                    pl.BlockSpec((d, chunk), lambda i, c: (0, c))],
        out_specs=pl.BlockSpec((tt, d), row),
        out_shape=jax.ShapeDtypeStruct((n, d), F32),
        scratch_shapes=[pltpu.VMEM((d, tt), BF16),
                        pltpu.VMEM((nhp * PEER_NKEYS, tt), BF16),
                        pltpu.VMEM((nhp, PEER_NKEYS, tt), F32),
                        pltpu.VMEM((PEER_HEADS, PEER_NKEYS, tt), F32),
                        pltpu.VMEM((PEER_HEADS, PEER_NKEYS, tt), F32),
                        pltpu.VMEM((PEER_HEADS, tt), F32),
                        pltpu.VMEM((2, PEER_TOPK + 8, tt), F32),
                        pltpu.VMEM((_PEER_CAND_ROWS, tt), F32),
                        pltpu.VMEM((chunk, tt), BF16),
                        pltpu.VMEM((d, tt), F32)],
        compiler_params=_params("parallel", "arbitrary"),
        name="peer",
    )(x2, shift, scale, gate, ln_g, ln_b, wqt, keys, u_tab, vt_tab)


def _rotate_half_cols(w, dim):
    rows, cols = w.shape
    w3 = w.reshape(rows, cols // dim, dim)
    return jnp.concatenate([-w3[..., dim // 2:], w3[..., :dim // 2]], axis=-1).reshape(rows, cols)


def _pad_cols(w, width):
    return jnp.pad(w, ((0, 0), (0, width - w.shape[1])))


def _in_proj_weights(w_in):
    widths = (512, 128, 128, 384, 256, 32, 512, 512, 512, 512, 512, 512)
    names = ("qa", "ka", "va", "cq", "ckv", "kr", "qc", "kc", "vc", "qd", "kd", "vd")
    parts, off = {}, 0
    for name, width in zip(names, widths):
        parts[name] = w_in[:, off:off + width]
        off += width
    for name, extra in (("qa", 1.0), ("qc", LOG2_E), ("qd", 1.0)):
        parts[name] = parts[name] * (HEAD_DIM ** -0.5 * extra)
    rot_dim = {"qa": HEAD_DIM, "ka": HEAD_DIM, "qc": DIFF_QK, "kc": DIFF_QK, "kr": MLA_ROPE}
    cols = [_pad_cols(parts[name], width) for name, width in _MAIN_GROUPS]
    cols += [_pad_cols(_rotate_half_cols(parts[name], rot_dim[name]), width) for name, width in _ROT_GROUPS]
    return jnp.concatenate(cols, axis=1).astype(BF16)


def _mla_weights(q_norm, q_up, kv_norm, kv_up):
    qr, kvr = q_up.shape[0], kv_up.shape[0]
    qh = q_up.reshape(qr, MLA_HEADS, MLA_NOPE + MLA_ROPE) * ((MLA_NOPE + MLA_ROPE) ** -0.5 * LOG2_E)
    pad = MLA_SLOT - MLA_NOPE - MLA_ROPE
    wq = jnp.pad(qh, ((0, 0), (0, 0), (0, pad))).reshape(qr, MLA_HEADS * MLA_SLOT)
    rot = _rotate_half_cols(qh[..., MLA_NOPE:].reshape(qr, MLA_HEADS * MLA_ROPE), MLA_ROPE)
    rot = rot.reshape(qr, MLA_HEADS, MLA_ROPE)
    wqr = jnp.pad(rot, ((0, 0), (0, 0), (MLA_NOPE, pad))).reshape(qr, MLA_HEADS * MLA_SLOT)
    kvh = kv_up.reshape(kvr, MLA_HEADS, MLA_NOPE + MLA_V)
    wk = jnp.pad(kvh[..., :MLA_NOPE], ((0, 0), (0, 0), (0, MLA_SLOT - MLA_NOPE))).reshape(kvr, MLA_HEADS * MLA_SLOT)
    wv = kvh[..., MLA_NOPE:].reshape(kvr, MLA_HEADS * MLA_V)
    place = np.zeros((128, MLA_HEADS, MLA_SLOT), np.float32)
    for r in range(MLA_ROPE):
        place[r, :, MLA_NOPE + r] = 1.0
    return {"qn": q_norm.reshape(1, qr), "kvn": kv_norm.reshape(1, kvr),
            "wq": wq.astype(BF16), "wqr": wqr.astype(BF16), "wk": wk.astype(BF16), "wv": wv.astype(BF16),
            "e": jnp.asarray(place.reshape(128, MLA_HEADS * MLA_SLOT), BF16)}


def _rope_tables(seq):
    def base(dim):
        inv = ROPE_THETA ** (-jnp.arange(0, dim, 2, dtype=F32) / dim)
        ang = jnp.arange(seq, dtype=F32)[:, None] * inv[None, :]
        cos, sin = jnp.cos(ang), jnp.sin(ang)
        return jnp.concatenate([cos, cos], axis=1), jnp.concatenate([sin, sin], axis=1)

    cos64, sin64 = base(HEAD_DIM)
    cos32, sin32 = base(MLA_ROPE)
    ones = jnp.ones((seq, MLA_NOPE), F32)
    zeros = jnp.zeros((seq, MLA_NOPE), F32)
    tail = jnp.zeros((seq, MLA_SLOT - MLA_NOPE - MLA_ROPE), F32)
    return {"cos64": jnp.concatenate([cos64, cos64], axis=1), "sin64": jnp.concatenate([sin64, sin64], axis=1),
            "cos32": _pad_cols(cos32, 128), "sin32": _pad_cols(sin32, 128),
            "mla_cos": jnp.concatenate([ones, cos32, tail], axis=1),
            "mla_sin": jnp.concatenate([zeros, sin32, tail], axis=1)}


def _tile(total, want):
    t = min(total, want)
    assert total % t == 0
    return t


def kernel(x, c, ada_w, ada_b, w_in, swa_sink, mla_q_norm, mla_q_up, mla_kv_norm, mla_kv_up, diff_lambda_q1, diff_lambda_k1, diff_lambda_q2, diff_lambda_k2, diff_subln, nat_rpb, w_gate, w_branch, w_out, ln1_g, ln1_b, peer_wq, peer_keys, peer_u, peer_v, ln2_g, ln2_b):
    batch, seq, d = x.shape
    depth = ada_w.shape[0]
    alpha = (2 * depth) ** 0.25
    n = batch * seq
    tabs = _rope_tables(seq)
    mod = _ada(c, ada_w, ada_b)
    tm = _tile(seq, 512)
    x2 = x.reshape(n, d)
    for l in range(depth):
        sh_mix, sc_mix, g_mix, sh_ffn, sc_ffn, g_ffn = (mod[l, :, j] for j in range(6))
        (qa, ka, va, cq, ckv, kr, qc, kc, vc, qd, kd, vd) = _in_proj(
            x2, sh_mix, sc_mix, _in_proj_weights(w_in[l]), tabs, seq, tm)
        o_a = _swa(qa, ka, va, swa_sink[l], batch, seq, _tile(seq, 256))
        qm, km, vm = _mla_prep(cq, ckv, kr, _mla_weights(mla_q_norm[l], mla_q_up[l], mla_kv_norm[l], mla_kv_up[l]),
                               tabs, seq, tm)
        o_b = _mla(qm, km, vm, batch, seq, _tile(seq, FLASH_TQ), _tile(seq, FLASH_TK))
        lambda_init = 0.8 - 0.6 * math.exp(-0.3 * l)
        lams = tuple(v[l].reshape(1, DIFF_QK) for v in (diff_lambda_q1, diff_lambda_k1, diff_lambda_q2, diff_lambda_k2))
        o_c = _diff(qc, kc, vc, lams, diff_subln[l].reshape(1, DIFF_V), lambda_init, batch, seq,
                    _tile(seq, FLASH_TQ), _tile(seq, FLASH_TK))
        o_d = _nat(qd, kd, vd, nat_rpb[l], batch, seq)
        x2 = _merge(x2, (o_a, o_b, o_c, o_d), sh_mix, sc_mix, g_mix, w_gate[l].astype(BF16),
                    w_branch[l].astype(BF16), w_out[l].astype(BF16), ln1_g[l].reshape(1, d), ln1_b[l].reshape(1, d),
                    alpha, seq, tm)
        keys = peer_keys[l].reshape(2 * PEER_HEADS, PEER_NKEYS, PEER_DKEY // 2).astype(BF16)
        x2 = _peer(x2, sh_ffn, sc_ffn, g_ffn, ln2_g[l].reshape(1, d), ln2_b[l].reshape(1, d),
                   peer_wq[l].T.astype(BF16), keys, peer_u[l].astype(BF16), peer_v[l].T.astype(BF16),
                   alpha, seq, tm)
    return x2.reshape(batch, seq, d)
```

```python
import functools
import math

import jax
import jax.numpy as jnp
import numpy as np
from jax import lax
from jax.experimental import pallas as pl
from jax.experimental.pallas import tpu as pltpu

F32 = jnp.float32
BF16 = jnp.bfloat16

GRID_W = 64
ROPE_THETA = 10000.0
HEAD_DIM = 64
LN_EPS = 1e-5
NEG_INF = -1e30
LOG2_E = 1.4426950408889634

SWA_HEADS = 8
SWA_KV_HEADS = 2
SWA_WINDOW = 128
MLA_HEADS = 8
MLA_NOPE = 64
MLA_ROPE = 32
MLA_V = 64
MLA_SLOT = 128
DIFF_HEADS = 4
DIFF_QK = 64
DIFF_V = 128
NAT_HEADS = 8
NAT_KR = 8
NAT_KC = 16
PEER_HEADS = 8
PEER_NKEYS = 128
PEER_DKEY = 256
PEER_TOPK = 16

VMEM_LIMIT_BYTES = 56 * 1024 * 1024
FLASH_TQ = 2048
FLASH_Q_BLOCKS = 4
FLASH_TK = 1024
NAT_ROWS_PER_STEP = 4
PEER_CHUNK = 2048
PEER_SUB = 512

_MAIN_GROUPS = (("qa", 512), ("ka", 128), ("va", 128), ("cq", 384), ("ckv", 256), ("qc", 512), ("kc", 512),
                ("vc", 512), ("qd", 512), ("kd", 512), ("vd", 512), ("kr", 128))
_ROT_GROUPS = (("qa", 512), ("ka", 128), ("qc", 512), ("kc", 512), ("kr", 128))


def _offsets(groups, base=0):
    out, off = {}, base
    for name, width in groups:
        out[name] = (off, off + width)
        off += width
    return out, off


_MAIN_OFF, _MAIN_END = _offsets(_MAIN_GROUPS)
_ROT_OFF, _W_ALL_COLS = _offsets(_ROT_GROUPS, _MAIN_END)

_NT = (((1,), (1,)), ((), ()))


def _params(*sem):
    return pltpu.CompilerParams(dimension_semantics=sem, vmem_limit_bytes=VMEM_LIMIT_BYTES)


def _norm_rows(x):
    mu = jnp.mean(x, axis=-1, keepdims=True)
    xc = x - mu
    var = jnp.mean(xc * xc, axis=-1, keepdims=True)
    return xc * lax.rsqrt(var + LN_EPS)


def _sigmoid(z):
    return 1.0 / (1.0 + jnp.exp(-z))


def _bdot(a, b):
    return jnp.dot(a, b, preferred_element_type=F32)


def _ada_kernel(c_ref, w_ref, b_ref, o_ref):
    c = c_ref[...]
    act = c * _sigmoid(c)
    o_ref[...] = jnp.dot(act, w_ref[...], preferred_element_type=F32,
                         precision=lax.Precision.HIGHEST) + b_ref[...]


def _ada(c, ada_w, ada_b):
    depth, d, six_d = ada_w.shape
    b = c.shape[0]
    rows = -(-b // 8) * 8
    c_pad = jnp.pad(c, ((0, rows - b), (0, 0)))
    out = pl.pallas_call(
        _ada_kernel,
        grid=(depth, six_d // d),
        in_specs=[pl.BlockSpec((rows, d), lambda l, j: (0, 0)),
                  pl.BlockSpec((None, d, d), lambda l, j: (l, 0, j)),
                  pl.BlockSpec((None, 1, d), lambda l, j: (l, 0, j))],
        out_specs=pl.BlockSpec((None, rows, d), lambda l, j: (l, 0, j)),
        out_shape=jax.ShapeDtypeStruct((depth, rows, six_d), F32),
        compiler_params=_params("parallel", "parallel"),
        name="ada",
    )(c_pad, ada_w, ada_b.reshape(depth, 1, six_d))
    return out[:, :b].reshape(depth, b, 6, 1, d)


def _in_proj_kernel(x_ref, sh_ref, sc_ref, w_ref, cos_ref, sin_ref, cos32_ref, sin32_ref,
                    qa_ref, ka_ref, va_ref, cq_ref, ckv_ref, kr_ref, qc_ref, kc_ref, vc_ref,
                    qd_ref, kd_ref, vd_ref):
    u = _norm_rows(x_ref[...]) * (1.0 + sc_ref[...]) + sh_ref[...]
    ub = u.astype(BF16)

    def proj(off):
        return _bdot(ub, w_ref[:, off[0]:off[1]])

    def roped(name, cos, sin):
        main, rot = proj(_MAIN_OFF[name]), proj(_ROT_OFF[name])
        reps = main.shape[1] // cos.shape[1]
        if reps > 1:
            cos = jnp.concatenate([cos] * reps, axis=1)
            sin = jnp.concatenate([sin] * reps, axis=1)
        return main * cos + rot * sin

    cos, sin = cos_ref[...], sin_ref[...]
    qa_ref[...] = roped("qa", cos, sin).astype(qa_ref.dtype)
    ka_ref[...] = roped("ka", cos, sin).astype(ka_ref.dtype)
    va_ref[...] = proj(_MAIN_OFF["va"]).astype(va_ref.dtype)
    cq_ref[...] = proj(_MAIN_OFF["cq"])
    ckv_ref[...] = proj(_MAIN_OFF["ckv"])
    kr_ref[...] = roped("kr", cos32_ref[...], sin32_ref[...]).astype(kr_ref.dtype)
    qc_ref[...] = roped("qc", cos, sin).astype(qc_ref.dtype)
    kc_ref[...] = roped("kc", cos, sin).astype(kc_ref.dtype)
    vc_ref[...] = proj(_MAIN_OFF["vc"]).astype(vc_ref.dtype)
    qd_ref[...] = proj(_MAIN_OFF["qd"]).astype(qd_ref.dtype)
    kd_ref[...] = proj(_MAIN_OFF["kd"]).astype(kd_ref.dtype)
    vd_ref[...] = proj(_MAIN_OFF["vd"]).astype(vd_ref.dtype)


def _in_proj(x2, shift, scale, w_all, tabs, seq, tm):
    n, d = x2.shape
    spt = seq // tm
    widths = dict(_MAIN_GROUPS)
    names = ("qa", "ka", "va", "cq", "ckv", "kr", "qc", "kc", "vc", "qd", "kd", "vd")
    dtypes = {k: BF16 for k in names}
    dtypes["cq"] = F32
    dtypes["ckv"] = F32
    row = lambda i: (i, 0)
    mod = lambda i: (i // spt, 0, 0)
    tab = lambda i: (i % spt, 0)
    return pl.pallas_call(
        _in_proj_kernel,
        grid=(n // tm,),
        in_specs=[pl.BlockSpec((tm, d), row),
                  pl.BlockSpec((None, 1, d), mod),
                  pl.BlockSpec((None, 1, d), mod),
                  pl.BlockSpec((d, _W_ALL_COLS), lambda i: (0, 0), pipeline_mode=pl.Buffered(1)),
                  pl.BlockSpec((tm, 128), tab), pl.BlockSpec((tm, 128), tab),
                  pl.BlockSpec((tm, 128), tab), pl.BlockSpec((tm, 128), tab)],
        out_specs=[pl.BlockSpec((tm, widths[k]), row) for k in names],
        out_shape=[jax.ShapeDtypeStruct((n, widths[k]), dtypes[k]) for k in names],
        compiler_params=_params("parallel"),
        name="in_proj",
    )(x2, shift, scale, w_all, tabs["cos64"], tabs["sin64"], tabs["cos32"], tabs["sin32"])


def _swa_kernel(q_ref, k_ref, v_ref, sink_ref, o_ref, *, tq, seq):
    i = pl.program_id(1)
    span = tq + 2 * SWA_WINDOW
    start = pl.multiple_of(jnp.clip(i * tq - SWA_WINDOW, 0, seq - span), SWA_WINDOW)
    ks = k_ref[pl.ds(start, span), :]
    vs = v_ref[pl.ds(start, span), :]
    qpos = i * tq + lax.broadcasted_iota(jnp.int32, (tq, span), 0)
    kpos = start + lax.broadcasted_iota(jnp.int32, (tq, span), 1)
    valid = jnp.abs(qpos - kpos) <= SWA_WINDOW
    q = q_ref[...]
    grp = SWA_HEADS // SWA_KV_HEADS
    outs = []
    for h in range(SWA_HEADS):
        kv = h // grp
        qh = q[:, h * HEAD_DIM:(h + 1) * HEAD_DIM]
        kh = ks[:, kv * HEAD_DIM:(kv + 1) * HEAD_DIM]
        vh = vs[:, kv * HEAD_DIM:(kv + 1) * HEAD_DIM]
        s = lax.dot_general(qh, kh, _NT, preferred_element_type=F32)
        s = jnp.where(valid, s, NEG_INF)
        sink = sink_ref[h][:, :1]
        m = jnp.maximum(jnp.max(s, axis=-1, keepdims=True), sink)
        p = jnp.exp(s - m)
        denom = jnp.sum(p, axis=-1, keepdims=True) + jnp.exp(sink - m)
        outs.append(_bdot(p.astype(BF16), vh) / denom)
    o_ref[...] = jnp.concatenate(outs, axis=1).astype(o_ref.dtype)


def _swa(qa, ka, va, sink, batch, seq, tq):
    n = qa.shape[0]
    nq = seq // tq
    sink_b = jnp.broadcast_to(sink.astype(F32)[:, None, None], (SWA_HEADS, 1, 128))
    return pl.pallas_call(
        functools.partial(_swa_kernel, tq=tq, seq=seq),
        grid=(batch, nq),
        in_specs=[pl.BlockSpec((tq, 512), lambda b, i: (b * nq + i, 0)),
                  pl.BlockSpec((seq, 128), lambda b, i: (b, 0)),
                  pl.BlockSpec((seq, 128), lambda b, i: (b, 0)),
                  pl.BlockSpec((SWA_HEADS, 1, 128), lambda b, i: (0, 0, 0))],
        out_specs=pl.BlockSpec((tq, 512), lambda b, i: (b * nq + i, 0)),
        out_shape=jax.ShapeDtypeStruct((n, 512), BF16),
        compiler_params=_params("parallel", "parallel"),
        name="swa",
    )(qa, ka, va, sink_b)


def _mla_prep_kernel(cq_ref, ckv_ref, kr_ref, qn_ref, kvn_ref, wq_ref, wqr_ref, wk_ref, e_ref, wv_ref,
                     ct_ref, st_ref, qm_ref, km_ref, vm_ref):
    def rms(x, g):
        return (x * lax.rsqrt(jnp.mean(x * x, axis=-1, keepdims=True) + LN_EPS) * g).astype(BF16)

    qn = rms(cq_ref[...], qn_ref[...])
    kvn = rms(ckv_ref[...], kvn_ref[...])
    cos = jnp.concatenate([ct_ref[...]] * MLA_HEADS, axis=1)
    sin = jnp.concatenate([st_ref[...]] * MLA_HEADS, axis=1)
    qm_ref[...] = (_bdot(qn, wq_ref[...]) * cos + _bdot(qn, wqr_ref[...]) * sin).astype(qm_ref.dtype)
    km_ref[...] = (_bdot(kvn, wk_ref[...]) + _bdot(kr_ref[...], e_ref[...])).astype(km_ref.dtype)
    vm_ref[...] = _bdot(kvn, wv_ref[...]).astype(vm_ref.dtype)


def _mla_prep(cq, ckv, kr, w, tabs, seq, tm):
    n = cq.shape[0]
    spt = seq // tm
    row = lambda i: (i, 0)
    full = lambda i: (0, 0)
    tab = lambda i: (i % spt, 0)
    hw = MLA_HEADS * MLA_SLOT
    return pl.pallas_call(
        _mla_prep_kernel,
        grid=(n // tm,),
        in_specs=[pl.BlockSpec((tm, cq.shape[1]), row), pl.BlockSpec((tm, ckv.shape[1]), row),
                  pl.BlockSpec((tm, 128), row),
                  pl.BlockSpec((1, cq.shape[1]), full), pl.BlockSpec((1, ckv.shape[1]), full),
                  pl.BlockSpec(w["wq"].shape, full), pl.BlockSpec(w["wqr"].shape, full),
                  pl.BlockSpec(w["wk"].shape, full), pl.BlockSpec(w["e"].shape, full),
                  pl.BlockSpec(w["wv"].shape, full),
                  pl.BlockSpec((tm, 128), tab), pl.BlockSpec((tm, 128), tab)],
        out_specs=[pl.BlockSpec((tm, hw), row), pl.BlockSpec((tm, hw), row),
                   pl.BlockSpec((tm, MLA_HEADS * MLA_V), row)],
        out_shape=[jax.ShapeDtypeStruct((n, hw), BF16), jax.ShapeDtypeStruct((n, hw), BF16),
                   jax.ShapeDtypeStruct((n, MLA_HEADS * MLA_V), BF16)],
        compiler_params=_params("parallel"),
        name="mla_prep",
    )(cq, ckv, kr, w["qn"], w["kvn"], w["wq"], w["wqr"], w["wk"], w["e"], w["wv"],
      tabs["mla_cos"], tabs["mla_sin"])


def _flash_chains(qs, k_ref, vt_ref, k_lanes, seq, tk):
    dv = vt_ref.shape[1]

    def step(kt, carry):
        off = pl.multiple_of(kt * tk, tk)
        vt = vt_ref[kt]
        out = []
        scores = [lax.dot_general(k_ref[pl.ds(off, tk), lanes[0]:lanes[1]], q, _NT, preferred_element_type=F32)
                  for q, lanes in zip(qs, k_lanes)]
        for s, (m, l, acc) in zip(scores, carry):
            m_new = jnp.maximum(m, jnp.max(s, axis=0, keepdims=True))
            alpha = jnp.exp2(m - m_new)
            p = jnp.exp2(s - m_new)
            l = alpha * l + jnp.sum(p, axis=0, keepdims=True)
            acc = alpha * acc + _bdot(vt, p.astype(BF16))
            out.append((m_new, l, acc))
        return tuple(out)

    init = tuple((jnp.full((1, q.shape[0]), NEG_INF, F32), jnp.zeros((1, q.shape[0]), F32),
                  jnp.zeros((dv, q.shape[0]), F32)) for q in qs)
    final = lax.fori_loop(0, seq // tk, step, init)
    return [(acc, l) for _, l, acc in final]


def _values_transposed(v, batch, seq, tk):
    groups = v.shape[1] // 128
    v5 = v.reshape(batch, seq // tk, tk, groups, 128)
    return jnp.transpose(v5, (0, 3, 1, 4, 2))


def _mla_kernel(q_ref, k_ref, vt_ref, o_ref, *, seq, tk):
    rows = q_ref.shape[0] // FLASH_Q_BLOCKS
    lanes = [(hh * MLA_SLOT, (hh + 1) * MLA_SLOT) for hh in range(2)]
    qs = [q_ref[r * rows:(r + 1) * rows, lo:hi] for r in range(FLASH_Q_BLOCKS) for lo, hi in lanes]
    outs = [acc / l for acc, l in _flash_chains(qs, k_ref, vt_ref, lanes * FLASH_Q_BLOCKS, seq, tk)]
    row = lax.broadcasted_iota(jnp.int32, outs[0].shape, 0)
    for r in range(FLASH_Q_BLOCKS):
        o_t = jnp.where(row < MLA_V, outs[2 * r], outs[2 * r + 1])
        o_ref[r * rows:(r + 1) * rows, :] = jnp.transpose(o_t).astype(o_ref.dtype)


def _mla(qm, km, vm, batch, seq, tq, tk):
    n = qm.shape[0]
    nq = seq // tq
    pairs = MLA_HEADS // 2
    return pl.pallas_call(
        functools.partial(_mla_kernel, seq=seq, tk=tk),
        grid=(batch, pairs, nq),
        in_specs=[pl.BlockSpec((tq, 2 * MLA_SLOT), lambda b, h, i: (b * nq + i, h)),
                  pl.BlockSpec((seq, 2 * MLA_SLOT), lambda b, h, i: (b, h)),
                  pl.BlockSpec((None, None, seq // tk, 2 * MLA_V, tk), lambda b, h, i: (b, h, 0, 0, 0))],
        out_specs=pl.BlockSpec((tq, 2 * MLA_V), lambda b, h, i: (b * nq + i, h)),
        out_shape=jax.ShapeDtypeStruct((n, MLA_HEADS * MLA_V), BF16),
        compiler_params=_params("parallel", "parallel", "parallel"),
        name="mla",
    )(qm, km, _values_transposed(vm, batch, seq, tk))


def _diff_kernel(q_ref, k_ref, vt_ref, lq1_ref, lk1_ref, lq2_ref, lk2_ref, sub_ref, o_ref, *,
                 seq, tk, lambda_init):
    rows = q_ref.shape[0] // FLASH_Q_BLOCKS
    lane = lax.broadcasted_iota(jnp.int32, (rows, 2 * DIFF_QK), 1)
    qs = []
    for r in range(FLASH_Q_BLOCKS):
        q = q_ref[r * rows:(r + 1) * rows, :]
        zero = jnp.zeros_like(q)
        qs += [jnp.where(lane < DIFF_QK, q, zero), jnp.where(lane >= DIFF_QK, q, zero)]
    res = _flash_chains(qs, k_ref, vt_ref, [(0, 2 * DIFF_QK)] * len(qs), seq, tk)
    lam = (jnp.exp(jnp.sum(lq1_ref[...] * lk1_ref[...], axis=-1, keepdims=True))
           - jnp.exp(jnp.sum(lq2_ref[...] * lk2_ref[...], axis=-1, keepdims=True)) + lambda_init)
    for r in range(FLASH_Q_BLOCKS):
        (acc1, l1), (acc2, l2) = res[2 * r], res[2 * r + 1]
        o = jnp.transpose(acc1 / l1 - lam * (acc2 / l2))
        o = o * lax.rsqrt(jnp.mean(o * o, axis=-1, keepdims=True) + LN_EPS) * sub_ref[...]
        o_ref[r * rows:(r + 1) * rows, :] = (o * (1.0 - lambda_init)).astype(o_ref.dtype)


def _diff(qc, kc, vc, lams, subln, lambda_init, batch, seq, tq, tk):
    n = qc.shape[0]
    nq = seq // tq
    vec = lambda b, h, i: (0, 0)
    return pl.pallas_call(
        functools.partial(_diff_kernel, seq=seq, tk=tk, lambda_init=lambda_init),
        grid=(batch, DIFF_HEADS, nq),
        in_specs=[pl.BlockSpec((tq, 2 * DIFF_QK), lambda b, h, i: (b * nq + i, h)),
                  pl.BlockSpec((seq, 2 * DIFF_QK), lambda b, h, i: (b, h)),
                  pl.BlockSpec((None, None, seq // tk, DIFF_V, tk), lambda b, h, i: (b, h, 0, 0, 0)),
                  pl.BlockSpec((1, DIFF_QK), vec), pl.BlockSpec((1, DIFF_QK), vec),
                  pl.BlockSpec((1, DIFF_QK), vec), pl.BlockSpec((1, DIFF_QK), vec),
                  pl.BlockSpec((1, DIFF_V), vec)],
        out_specs=pl.BlockSpec((tq, DIFF_V), lambda b, h, i: (b * nq + i, h)),
        out_shape=jax.ShapeDtypeStruct((n, DIFF_HEADS * DIFF_V), BF16),
        compiler_params=_params("parallel", "parallel", "parallel"),
        name="diff",
    )(qc, kc, _values_transposed(vc, batch, seq, tk), *lams, subln)


def _nat_row_start(r, rows):
    return jnp.clip(r - NAT_KR // 2, 0, rows - NAT_KR)


def _nat_kernel(q_ref, k_ref, v_ref, bias_ref, o_ref, *, rows, rows_per_step):
    heads = [slice(h * HEAD_DIM, (h + 1) * HEAD_DIM) for h in range(NAT_HEADS)]
    values, logits = [], []
    for rr in range(rows_per_step):
        r = pl.program_id(1) * rows_per_step + rr
        r0 = _nat_row_start(r, rows)
        band = r0 - r + (NAT_KR - 1)
        off = pl.multiple_of(r0 * GRID_W, GRID_W)
        ks = k_ref[pl.ds(off, NAT_KR * GRID_W), :]
        values.append(v_ref[pl.ds(off, NAT_KR * GRID_W), :])
        q = q_ref[rr * GRID_W:(rr + 1) * GRID_W, :]
        logits.append([lax.dot_general(q[:, sl], ks[:, sl], _NT, preferred_element_type=F32) + bias_ref[band, h]
                       for h, sl in enumerate(heads)])
    probs = []
    for row_logits in logits:
        row_probs = []
        for s in row_logits:
            p = jnp.exp(s - jnp.max(s, axis=-1, keepdims=True))
            row_probs.append((p.astype(BF16), jnp.sum(p, axis=-1, keepdims=True)))
        probs.append(row_probs)
    for rr, (row_probs, vs) in enumerate(zip(probs, values)):
        outs = [_bdot(p, vs[:, sl]) / denom for (p, denom), sl in zip(row_probs, heads)]
        o_ref[rr * GRID_W:(rr + 1) * GRID_W, :] = jnp.concatenate(outs, axis=1).astype(o_ref.dtype)


def _nat_bias_table(rpb):
    heads, n_row_off, n_col_off = rpb.shape
    cols = np.arange(GRID_W)
    col_start = np.clip(cols - NAT_KC // 2, 0, GRID_W - NAT_KC)
    inside = (cols[None, :] >= col_start[:, None]) & (cols[None, :] < col_start[:, None] + NAT_KC)
    period = 2 * GRID_W
    lead = GRID_W - NAT_KC
    sig = jnp.pad(rpb.astype(F32), ((0, 0), (0, 0), (lead, period - n_col_off - lead)))
    flat = jnp.tile(sig, (1, 1, GRID_W))[..., :GRID_W * (period - 1)]
    toep = flat.reshape(heads, n_row_off, GRID_W, period - 1)[..., GRID_W - 1:]
    toep = jnp.where(inside[None, None], toep, NEG_INF)
    bands = [jnp.transpose(toep[:, d:d + NAT_KR], (0, 2, 1, 3)).reshape(heads, GRID_W, NAT_KR * GRID_W)
             for d in range(NAT_KR)]
    return jnp.stack(bands, axis=0)


def _nat(qd, kd, vd, rpb, batch, seq):
    n = qd.shape[0]
    rows = seq // GRID_W
    assert rows >= NAT_KR
    bias = _nat_bias_table(rpb)
    rps = NAT_ROWS_PER_STEP
    assert rows % rps == 0
    steps = rows // rps
    once = pl.Buffered(1)
    return pl.pallas_call(
        functools.partial(_nat_kernel, rows=rows, rows_per_step=rps),
        grid=(batch, steps),
        in_specs=[pl.BlockSpec((rps * GRID_W, 512), lambda b, r: (b * steps + r, 0)),
                  pl.BlockSpec((seq, 512), lambda b, r: (b, 0), pipeline_mode=once),
                  pl.BlockSpec((seq, 512), lambda b, r: (b, 0), pipeline_mode=once),
                  pl.BlockSpec(bias.shape, lambda b, r: (0, 0, 0, 0), pipeline_mode=once)],
        out_specs=pl.BlockSpec((rps * GRID_W, 512), lambda b, r: (b * steps + r, 0)),
        out_shape=jax.ShapeDtypeStruct((n, 512), BF16),
        compiler_params=_params("parallel", "arbitrary"),
        name="nat",
    )(qd, kd, vd, bias)


def _merge_kernel(x_ref, oa_ref, ob_ref, oc_ref, od_ref, sh_ref, sc_ref, gm_ref, wg_ref, wb_ref, wo_ref,
                  g_ref, b_ref, o_ref, *, alpha):
    x = x_ref[...]
    ub = (_norm_rows(x) * (1.0 + sc_ref[...]) + sh_ref[...]).astype(BF16)
    merged = None
    for n, br_ref in enumerate((oa_ref, ob_ref, oc_ref, od_ref)):
        term = _sigmoid(_bdot(ub, wg_ref[n])) * _bdot(br_ref[...], wb_ref[n])
        merged = term if merged is None else merged + term
    y = _bdot(merged.astype(BF16), wo_ref[...])
    z = alpha * x + gm_ref[...] * y
    o_ref[...] = _norm_rows(z) * g_ref[...] + b_ref[...]


def _merge(x2, branches, shift, scale, gate, wg, wb, wo, ln_g, ln_b, alpha, seq, tm):
    n, d = x2.shape
    spt = seq // tm
    row = lambda i: (i, 0)
    mod = lambda i: (i // spt, 0, 0)
    once = pl.Buffered(1)
    return pl.pallas_call(
        functools.partial(_merge_kernel, alpha=alpha),
        grid=(n // tm,),
        in_specs=[pl.BlockSpec((tm, d), row)] + [pl.BlockSpec((tm, 512), row)] * 4
                 + [pl.BlockSpec((None, 1, d), mod)] * 3
                 + [pl.BlockSpec(wg.shape, lambda i: (0, 0, 0), pipeline_mode=once),
                    pl.BlockSpec(wb.shape, lambda i: (0, 0, 0), pipeline_mode=once),
                    pl.BlockSpec(wo.shape, lambda i: (0, 0), pipeline_mode=once),
                    pl.BlockSpec((1, d), lambda i: (0, 0)), pl.BlockSpec((1, d), lambda i: (0, 0))],
        out_specs=pl.BlockSpec((tm, d), row),
        out_shape=jax.ShapeDtypeStruct((n, d), F32),
        compiler_params=_params("parallel"),
        name="merge",
    )(x2, *branches, shift, scale, gate, wg, wb, wo, ln_g, ln_b)


def _peer_candidate_blocks(k):
    blocks = []
    for i in range(k):
        need = min(k, (k + 1) // (i + 1))
        if need == 1:
            assert (k - i) % 8 == 0
            blocks.append((i, 1))
            break
        blocks.append((i, -(-need // 8) * 8))
    return tuple(blocks)


_PEER_CAND_BLOCKS = _peer_candidate_blocks(PEER_TOPK)
_PEER_CAND_ROWS = sum((PEER_TOPK - i) if w == 1 else w for i, w in _PEER_CAND_BLOCKS)


def _take_top(w, count, on_max, with_rank=False):
    rank = jnp.full(w.shape, float(count), F32) if with_rank else None
    for r in range(count):
        m = jnp.max(w, axis=0, keepdims=True)
        on_max(r, m)
        if r + 1 < count or with_rank:
            hit = w == m
            if with_rank:
                rank = jnp.where(hit, float(r), rank)
            if r + 1 < count:
                w = jnp.where(hit, -jnp.inf, w)
    return rank


def _peer_kernel(x_ref, sh_ref, sc_ref, gf_ref, g_ref, b_ref, wqt_ref, keys_ref, u_ref, vt_ref, o_ref,
                 ub_scr, q_scr, cnt_scr, rank_scr, p1_scr, p2_scr, top_scr, cand_scr, w_scr, acc_scr, *,
                 alpha, ipc, sub):
    c = pl.program_id(1)
    nk = PEER_NKEYS
    k = PEER_TOPK

    @pl.when(c == 0)
    def _scores_and_thresholds():
        u = _norm_rows(x_ref[...]) * (1.0 + sc_ref[...]) + sh_ref[...]
        ub_scr[...] = jnp.transpose(u).astype(BF16)
        q_scr[...] = _bdot(wqt_ref[...], ub_scr[...]).astype(BF16)

        def head(h, carry):
            halves = []
            for p in range(2):
                idx = 2 * h + p
                rows = pl.multiple_of(idx * nk, nk)
                s = _bdot(keys_ref[idx], q_scr[pl.ds(rows, nk), :])
                halves.append(s)

                def keep(r, m, p=p):
                    top_scr[p, r:r + 1, :] = m

                rank = _take_top(s, k + 1, keep, with_rank=(p == 1))
            s1, s2 = halves
            a = top_scr[0, :k, :]
            b = top_scr[1, :k, :]
            row = 0
            for i, width in _PEER_CAND_BLOCKS:
                if width == 1:
                    cand_scr[row:row + k - i, :] = a[i:k, :] + b[0:1, :]
                    row += k - i
                else:
                    cand_scr[row:row + width, :] = a[i:i + 1, :] + b[0:width, :]
                    row += width
            best = a[0:1, :] + b[0:1, :]
            stats = {"z": jnp.zeros_like(best)}

            def tally(r, m):
                if r < k:
                    stats["z"] = stats["z"] + jnp.exp(m - best)
                if r == k - 1:
                    stats["last"] = m
                if r == k:
                    stats["next"] = m

            _take_top(cand_scr[...], k + 1, tally)
            runner_up = jnp.maximum(stats["next"], jnp.maximum(top_scr[0, k:k + 1, :] + b[0:1, :],
                                                               a[0:1, :] + top_scr[1, k:k + 1, :]))
            bound = 0.5 * (stats["last"] + runner_up) - s1
            count = jnp.zeros_like(s1)
            for qq in range(k):
                count = jnp.where(b[qq:qq + 1, :] >= bound, float(qq + 1), count)
            cnt_scr[h] = count
            rank_scr[h] = rank.astype(BF16)
            p1_scr[h] = jnp.exp(s1 - a[0:1, :])
            p2_scr[h] = (jnp.exp(s2 - b[0:1, :]) / stats["z"]).astype(BF16)
            return carry

        lax.fori_loop(0, PEER_HEADS, head, 0)
        acc_scr[...] = jnp.zeros_like(acc_scr)

    ips = sub // nk
    n_sub = ipc // ips

    def hidden(sb):
        return _bdot(u_ref[sb * sub:(sb + 1) * sub, :], ub_scr[...])

    partial_out = None
    hids = [hidden(sb) for sb in range(n_sub)]
    for sb in range(n_sub):
        rows = slice(sb * sub, (sb + 1) * sub)
        hid = hids[sb]
        for il in range(ips):
            i = c * ipc + sb * ips + il
            gate = None
            for h in range(PEER_HEADS):
                cnt_row = cnt_scr[h, pl.ds(i, 1), :].astype(BF16)
                p1_row = p1_scr[h, pl.ds(i, 1), :].astype(BF16)
                chosen = rank_scr[h] < cnt_row
                term = jnp.where(chosen, p2_scr[h], 0.0) * p1_row
                gate = term if gate is None else gate + term
            hb = hid[il * nk:(il + 1) * nk, :]
            act = 0.5 * hb * (1.0 + lax.erf(hb * (2.0 ** -0.5)))
            w_scr[sb * sub + il * nk:sb * sub + (il + 1) * nk, :] = gate * act.astype(BF16)
        out = _bdot(vt_ref[:, rows], w_scr[rows, :])
        partial_out = out if partial_out is None else partial_out + out
    acc_scr[...] += partial_out

    @pl.when(c == pl.num_programs(1) - 1)
    def _finish():
        y = jnp.transpose(acc_scr[...])
        z = alpha * x_ref[...] + gf_ref[...] * y
        o_ref[...] = _norm_rows(z) * g_ref[...] + b_ref[...]


def _peer(x2, shift, scale, gate, ln_g, ln_b, wqt, keys, u_tab, vt_tab, alpha, seq, tt):
    n, d = x2.shape
    chunk, sub = PEER_CHUNK, PEER_SUB
    assert u_tab.shape[0] % chunk == 0 and chunk % sub == 0 and sub % PEER_NKEYS == 0
    spt = seq // tt
    experts = u_tab.shape[0]
    ipc = chunk // PEER_NKEYS
    nhp = 2 * PEER_HEADS
    row = lambda i, c: (i, 0)
    mod = lambda i, c: (i // spt, 0, 0)
    return pl.pallas_call(
        functools.partial(_peer_kernel, alpha=alpha, ipc=ipc, sub=sub),
        grid=(n // tt, experts // chunk),
        in_specs=[pl.BlockSpec((tt, d), row)] + [pl.BlockSpec((None, 1, d), mod)] * 3
                 + [pl.BlockSpec((1, d), lambda i, c: (0, 0)), pl.BlockSpec((1, d), lambda i, c: (0, 0)),
                    pl.BlockSpec(wqt.shape, lambda i, c: (0, 0), pipeline_mode=pl.Buffered(1)),
                    pl.BlockSpec(keys.shape, lambda i, c: (0, 0, 0), pipeline_mode=pl.Buffered(1)),
                    pl.BlockSpec((chunk, d), lambda i, c: (c, 0)),
                    pl.BlockSpec((d, chunk), lambda i, c: (0, c))],
        out_specs=pl.BlockSpec((tt, d), row),
        out_shape=jax.ShapeDtypeStruct((n, d), F32),
        scratch_shapes=[pltpu.VMEM((d, tt), BF16),
                        pltpu.VMEM((nhp * PEER_NKEYS, tt), BF16),
                        pltpu.VMEM((PEER_HEADS, PEER_NKEYS, tt), F32),
                        pltpu.VMEM((PEER_HEADS, PEER_NKEYS, tt), BF16),
                        pltpu.VMEM((PEER_HEADS, PEER_NKEYS, tt), F32),
                        pltpu.VMEM((PEER_HEADS, PEER_NKEYS, tt), BF16),
                        pltpu.VMEM((2, PEER_TOPK + 8, tt), F32),
                        pltpu.VMEM((_PEER_CAND_ROWS, tt), F32),
                        pltpu.VMEM((chunk, tt), BF16),
                        pltpu.VMEM((d, tt), F32)],
        compiler_params=_params("parallel", "arbitrary"),
        name="peer",
    )(x2, shift, scale, gate, ln_g, ln_b, wqt, keys, u_tab, vt_tab)


def _rotate_half_cols(w, dim):
    rows, cols = w.shape
    w3 = w.reshape(rows, cols // dim, dim)
    return jnp.concatenate([-w3[..., dim // 2:], w3[..., :dim // 2]], axis=-1).reshape(rows, cols)


def _pad_cols(w, width):
    return jnp.pad(w, ((0, 0), (0, width - w.shape[1])))


def _in_proj_weights(w_in):
    widths = (512, 128, 128, 384, 256, 32, 512, 512, 512, 512, 512, 512)
    names = ("qa", "ka", "va", "cq", "ckv", "kr", "qc", "kc", "vc", "qd", "kd", "vd")
    parts, off = {}, 0
    for name, width in zip(names, widths):
        parts[name] = w_in[:, off:off + width]
        off += width
    for name, extra in (("qa", 1.0), ("qc", LOG2_E), ("qd", 1.0)):
        parts[name] = parts[name] * (HEAD_DIM ** -0.5 * extra)
    rot_dim = {"qa": HEAD_DIM, "ka": HEAD_DIM, "qc": DIFF_QK, "kc": DIFF_QK, "kr": MLA_ROPE}
    cols = [_pad_cols(parts[name], width) for name, width in _MAIN_GROUPS]
    cols += [_pad_cols(_rotate_half_cols(parts[name], rot_dim[name]), width) for name, width in _ROT_GROUPS]
    return jnp.concatenate(cols, axis=1).astype(BF16)


def _mla_weights(q_norm, q_up, kv_norm, kv_up):
    qr, kvr = q_up.shape[0], kv_up.shape[0]
    qh = q_up.reshape(qr, MLA_HEADS, MLA_NOPE + MLA_ROPE) * ((MLA_NOPE + MLA_ROPE) ** -0.5 * LOG2_E)
    pad = MLA_SLOT - MLA_NOPE - MLA_ROPE
    wq = jnp.pad(qh, ((0, 0), (0, 0), (0, pad))).reshape(qr, MLA_HEADS * MLA_SLOT)
    rot = _rotate_half_cols(qh[..., MLA_NOPE:].reshape(qr, MLA_HEADS * MLA_ROPE), MLA_ROPE)
    rot = rot.reshape(qr, MLA_HEADS, MLA_ROPE)
    wqr = jnp.pad(rot, ((0, 0), (0, 0), (MLA_NOPE, pad))).reshape(qr, MLA_HEADS * MLA_SLOT)
    kvh = kv_up.reshape(kvr, MLA_HEADS, MLA_NOPE + MLA_V)
    wk = jnp.pad(kvh[..., :MLA_NOPE], ((0, 0), (0, 0), (0, MLA_SLOT - MLA_NOPE))).reshape(kvr, MLA_HEADS * MLA_SLOT)
    wv = kvh[..., MLA_NOPE:].reshape(kvr, MLA_HEADS * MLA_V)
    place = np.zeros((128, MLA_HEADS, MLA_SLOT), np.float32)
    for r in range(MLA_ROPE):
        place[r, :, MLA_NOPE + r] = 1.0
    return {"qn": q_norm.reshape(1, qr), "kvn": kv_norm.reshape(1, kvr),
            "wq": wq.astype(BF16), "wqr": wqr.astype(BF16), "wk": wk.astype(BF16), "wv": wv.astype(BF16),
            "e": jnp.asarray(place.reshape(128, MLA_HEADS * MLA_SLOT), BF16)}


def _rope_tables(seq):
    def base(dim):
        inv = ROPE_THETA ** (-jnp.arange(0, dim, 2, dtype=F32) / dim)
        ang = jnp.arange(seq, dtype=F32)[:, None] * inv[None, :]
        cos, sin = jnp.cos(ang), jnp.sin(ang)
        return jnp.concatenate([cos, cos], axis=1), jnp.concatenate([sin, sin], axis=1)

    cos64, sin64 = base(HEAD_DIM)
    cos32, sin32 = base(MLA_ROPE)
    ones = jnp.ones((seq, MLA_NOPE), F32)
    zeros = jnp.zeros((seq, MLA_NOPE), F32)
    tail = jnp.zeros((seq, MLA_SLOT - MLA_NOPE - MLA_ROPE), F32)
    return {"cos64": jnp.concatenate([cos64, cos64], axis=1), "sin64": jnp.concatenate([sin64, sin64], axis=1),
            "cos32": _pad_cols(cos32, 128), "sin32": _pad_cols(sin32, 128),
            "mla_cos": jnp.concatenate([ones, cos32, tail], axis=1),
            "mla_sin": jnp.concatenate([zeros, sin32, tail], axis=1)}


def _tile(total, want):
    t = min(total, want)
    assert total % t == 0
    return t


def kernel(x, c, ada_w, ada_b, w_in, swa_sink, mla_q_norm, mla_q_up, mla_kv_norm, mla_kv_up, diff_lambda_q1, diff_lambda_k1, diff_lambda_q2, diff_lambda_k2, diff_subln, nat_rpb, w_gate, w_branch, w_out, ln1_g, ln1_b, peer_wq, peer_keys, peer_u, peer_v, ln2_g, ln2_b):
    batch, seq, d = x.shape
    depth = ada_w.shape[0]
    alpha = (2 * depth) ** 0.25
    n = batch * seq
    tabs = _rope_tables(seq)
    mod = _ada(c, ada_w, ada_b)
    tm = _tile(seq, 512)
    x2 = x.reshape(n, d)
    for l in range(depth):
        sh_mix, sc_mix, g_mix, sh_ffn, sc_ffn, g_ffn = (mod[l, :, j] for j in range(6))
        (qa, ka, va, cq, ckv, kr, qc, kc, vc, qd, kd, vd) = _in_proj(
            x2, sh_mix, sc_mix, _in_proj_weights(w_in[l]), tabs, seq, tm)
        o_a = _swa(qa, ka, va, swa_sink[l], batch, seq, _tile(seq, 256))
        qm, km, vm = _mla_prep(cq, ckv, kr, _mla_weights(mla_q_norm[l], mla_q_up[l], mla_kv_norm[l], mla_kv_up[l]),
                               tabs, seq, tm)
        o_b = _mla(qm, km, vm, batch, seq, _tile(seq, FLASH_TQ), _tile(seq, FLASH_TK))
        lambda_init = 0.8 - 0.6 * math.exp(-0.3 * l)
        lams = tuple(v[l].reshape(1, DIFF_QK) for v in (diff_lambda_q1, diff_lambda_k1, diff_lambda_q2, diff_lambda_k2))
        o_c = _diff(qc, kc, vc, lams, diff_subln[l].reshape(1, DIFF_V), lambda_init, batch, seq,
                    _tile(seq, FLASH_TQ), _tile(seq, FLASH_TK))
        o_d = _nat(qd, kd, vd, nat_rpb[l], batch, seq)
        x2 = _merge(x2, (o_a, o_b, o_c, o_d), sh_mix, sc_mix, g_mix, w_gate[l].astype(BF16),
                    w_branch[l].astype(BF16), w_out[l].astype(BF16), ln1_g[l].reshape(1, d), ln1_b[l].reshape(1, d),
                    alpha, seq, tm)
        keys = peer_keys[l].reshape(2 * PEER_HEADS, PEER_NKEYS, PEER_DKEY // 2).astype(BF16)
        x2 = _peer(x2, sh_ffn, sc_ffn, g_ffn, ln2_g[l].reshape(1, d), ln2_b[l].reshape(1, d),
                   peer_wq[l].T.astype(BF16), keys, peer_u[l].astype(BF16), peer_v[l].T.astype(BF16),
                   alpha, seq, tm)
    return x2.reshape(batch, seq, d)
```

```python
import functools
import math

import jax
import jax.numpy as jnp
import numpy as np
from jax import lax
from jax.experimental import pallas as pl
from jax.experimental.pallas import tpu as pltpu

F32 = jnp.float32
BF16 = jnp.bfloat16

GRID_W = 64
ROPE_THETA = 10000.0
HEAD_DIM = 64
LN_EPS = 1e-5
NEG_INF = -1e30
LOG2_E = 1.4426950408889634

SWA_HEADS = 8
SWA_KV_HEADS = 2
SWA_WINDOW = 128
MLA_HEADS = 8
MLA_NOPE = 64
MLA_ROPE = 32
MLA_V = 64
MLA_SLOT = 128
DIFF_HEADS = 4
DIFF_QK = 64
DIFF_V = 128
NAT_HEADS = 8
NAT_KR = 8
NAT_KC = 16
PEER_HEADS = 8
PEER_NKEYS = 128
PEER_DKEY = 256
PEER_TOPK = 16

VMEM_LIMIT_BYTES = 56 * 1024 * 1024
FLASH_TQ = 2048
FLASH_Q_BLOCKS = 4
FLASH_TK = 1024
SWA_TQ = 256
NAT_ROWS_PER_STEP = 4
PEER_CHUNK = 2048
PEER_SUB = 512

_MAIN_GROUPS = (("qa", 512), ("ka", 128), ("va", 128), ("cq", 384), ("ckv", 256), ("qc", 512), ("kc", 512),
                ("vc", 512), ("qd", 512), ("kd", 512), ("vd", 512), ("kr", 128))
_ROT_GROUPS = (("qa", 512), ("ka", 128), ("qc", 512), ("kc", 512), ("kr", 128))


def _offsets(groups, base=0):
    out, off = {}, base
    for name, width in groups:
        out[name] = (off, off + width)
        off += width
    return out, off


_MAIN_OFF, _MAIN_END = _offsets(_MAIN_GROUPS)
_ROT_OFF, _W_ALL_COLS = _offsets(_ROT_GROUPS, _MAIN_END)

_NT = (((1,), (1,)), ((), ()))


def _params(*sem):
    return pltpu.CompilerParams(dimension_semantics=sem, vmem_limit_bytes=VMEM_LIMIT_BYTES)


def _norm_rows(x):
    mu = jnp.mean(x, axis=-1, keepdims=True)
    xc = x - mu
    var = jnp.mean(xc * xc, axis=-1, keepdims=True)
    return xc * lax.rsqrt(var + LN_EPS)


def _sigmoid(z):
    return 1.0 / (1.0 + jnp.exp(-z))


def _bdot(a, b):
    return jnp.dot(a, b, preferred_element_type=F32)


def _ada_kernel(c_ref, w_ref, b_ref, o_ref):
    c = c_ref[...]
    act = c * _sigmoid(c)
    o_ref[...] = jnp.dot(act, w_ref[...], preferred_element_type=F32,
                         precision=lax.Precision.HIGHEST) + b_ref[...]


def _ada(c, ada_w, ada_b):
    depth, d, six_d = ada_w.shape
    b = c.shape[0]
    rows = -(-b // 8) * 8
    c_pad = jnp.pad(c, ((0, rows - b), (0, 0)))
    out = pl.pallas_call(
        _ada_kernel,
        grid=(depth, six_d // d),
        in_specs=[pl.BlockSpec((rows, d), lambda l, j: (0, 0)),
                  pl.BlockSpec((None, d, d), lambda l, j: (l, 0, j)),
                  pl.BlockSpec((None, 1, d), lambda l, j: (l, 0, j))],
        out_specs=pl.BlockSpec((None, rows, d), lambda l, j: (l, 0, j)),
        out_shape=jax.ShapeDtypeStruct((depth, rows, six_d), F32),
        compiler_params=_params("parallel", "parallel"),
        name="ada",
    )(c_pad, ada_w, ada_b.reshape(depth, 1, six_d))
    return out[:, :b].reshape(depth, b, 6, 1, d)


def _in_proj_kernel(x_ref, sh_ref, sc_ref, w_ref, cos_ref, sin_ref, cos32_ref, sin32_ref,
                    qa_ref, ka_ref, va_ref, cq_ref, ckv_ref, kr_ref, qc_ref, kc_ref, vc_ref,
                    qd_ref, kd_ref, vd_ref):
    u = _norm_rows(x_ref[...]) * (1.0 + sc_ref[...]) + sh_ref[...]
    ub = u.astype(BF16)

    def proj(off):
        return _bdot(ub, w_ref[:, off[0]:off[1]])

    def roped(name, cos, sin):
        main, rot = proj(_MAIN_OFF[name]), proj(_ROT_OFF[name])
        reps = main.shape[1] // cos.shape[1]
        if reps > 1:
            cos = jnp.concatenate([cos] * reps, axis=1)
            sin = jnp.concatenate([sin] * reps, axis=1)
        return main * cos + rot * sin

    cos, sin = cos_ref[...], sin_ref[...]
    qa_ref[...] = roped("qa", cos, sin).astype(qa_ref.dtype)
    ka_ref[...] = roped("ka", cos, sin).astype(ka_ref.dtype)
    va_ref[...] = proj(_MAIN_OFF["va"]).astype(va_ref.dtype)
    cq_ref[...] = proj(_MAIN_OFF["cq"])
    ckv_ref[...] = proj(_MAIN_OFF["ckv"])
    kr_ref[...] = roped("kr", cos32_ref[...], sin32_ref[...]).astype(kr_ref.dtype)
    qc_ref[...] = roped("qc", cos, sin).astype(qc_ref.dtype)
    kc_ref[...] = roped("kc", cos, sin).astype(kc_ref.dtype)
    vc_ref[...] = proj(_MAIN_OFF["vc"]).astype(vc_ref.dtype)
    qd_ref[...] = proj(_MAIN_OFF["qd"]).astype(qd_ref.dtype)
    kd_ref[...] = proj(_MAIN_OFF["kd"]).astype(kd_ref.dtype)
    vd_ref[...] = proj(_MAIN_OFF["vd"]).astype(vd_ref.dtype)


def _in_proj(x2, shift, scale, w_all, tabs, seq, tm):
    n, d = x2.shape
    spt = seq // tm
    widths = dict(_MAIN_GROUPS)
    names = ("qa", "ka", "va", "cq", "ckv", "kr", "qc", "kc", "vc", "qd", "kd", "vd")
    dtypes = {k: BF16 for k in names}
    dtypes["cq"] = F32
    dtypes["ckv"] = F32
    row = lambda i: (i, 0)
    mod = lambda i: (i // spt, 0, 0)
    tab = lambda i: (i % spt, 0)
    return pl.pallas_call(
        _in_proj_kernel,
        grid=(n // tm,),
        in_specs=[pl.BlockSpec((tm, d), row),
                  pl.BlockSpec((None, 1, d), mod),
                  pl.BlockSpec((None, 1, d), mod),
                  pl.BlockSpec((d, _W_ALL_COLS), lambda i: (0, 0), pipeline_mode=pl.Buffered(1)),
                  pl.BlockSpec((tm, 128), tab), pl.BlockSpec((tm, 128), tab),
                  pl.BlockSpec((tm, 128), tab), pl.BlockSpec((tm, 128), tab)],
        out_specs=[pl.BlockSpec((tm, widths[k]), row) for k in names],
        out_shape=[jax.ShapeDtypeStruct((n, widths[k]), dtypes[k]) for k in names],
        compiler_params=_params("parallel"),
        name="in_proj",
    )(x2, shift, scale, w_all, tabs["cos64"], tabs["sin64"], tabs["cos32"], tabs["sin32"])


def _swa_kernel(q_ref, k_ref, v_ref, sink_ref, o_ref, *, tq, seq):
    i = pl.program_id(1)
    span = tq + 2 * SWA_WINDOW
    start = pl.multiple_of(jnp.clip(i * tq - SWA_WINDOW, 0, seq - span), SWA_WINDOW)
    ks = k_ref[pl.ds(start, span), :]
    vs = v_ref[pl.ds(start, span), :]
    qpos = i * tq + lax.broadcasted_iota(jnp.int32, (tq, span), 0)
    kpos = start + lax.broadcasted_iota(jnp.int32, (tq, span), 1)
    valid = jnp.abs(qpos - kpos) <= SWA_WINDOW
    q = q_ref[...]
    grp = SWA_HEADS // SWA_KV_HEADS
    outs = []
    for h in range(SWA_HEADS):
        kv = h // grp
        qh = q[:, h * HEAD_DIM:(h + 1) * HEAD_DIM]
        kh = ks[:, kv * HEAD_DIM:(kv + 1) * HEAD_DIM]
        vh = vs[:, kv * HEAD_DIM:(kv + 1) * HEAD_DIM]
        s = lax.dot_general(qh, kh, _NT, preferred_element_type=F32)
        s = jnp.where(valid, s, NEG_INF)
        sink = sink_ref[h][:, :1]
        m = jnp.maximum(jnp.max(s, axis=-1, keepdims=True), sink)
        p = jnp.exp(s - m)
        denom = jnp.sum(p, axis=-1, keepdims=True) + jnp.exp(sink - m)
        outs.append(_bdot(p.astype(BF16), vh) / denom)
    o_ref[...] = jnp.concatenate(outs, axis=1).astype(o_ref.dtype)


def _swa(qa, ka, va, sink, batch, seq, tq):
    n = qa.shape[0]
    nq = seq // tq
    sink_b = jnp.broadcast_to(sink.astype(F32)[:, None, None], (SWA_HEADS, 1, 128))
    return pl.pallas_call(
        functools.partial(_swa_kernel, tq=tq, seq=seq),
        grid=(batch, nq),
        in_specs=[pl.BlockSpec((tq, 512), lambda b, i: (b * nq + i, 0)),
                  pl.BlockSpec((seq, 128), lambda b, i: (b, 0)),
                  pl.BlockSpec((seq, 128), lambda b, i: (b, 0)),
                  pl.BlockSpec((SWA_HEADS, 1, 128), lambda b, i: (0, 0, 0))],
        out_specs=pl.BlockSpec((tq, 512), lambda b, i: (b * nq + i, 0)),
        out_shape=jax.ShapeDtypeStruct((n, 512), BF16),
        compiler_params=_params("parallel", "parallel"),
        name="swa",
    )(qa, ka, va, sink_b)


def _mla_prep_kernel(cq_ref, ckv_ref, kr_ref, qn_ref, kvn_ref, wq_ref, wqr_ref, wk_ref, e_ref, wv_ref,
                     ct_ref, st_ref, qm_ref, km_ref, vm_ref):
    def rms(x, g):
        return (x * lax.rsqrt(jnp.mean(x * x, axis=-1, keepdims=True) + LN_EPS) * g).astype(BF16)

    qn = rms(cq_ref[...], qn_ref[...])
    kvn = rms(ckv_ref[...], kvn_ref[...])
    cos = jnp.concatenate([ct_ref[...]] * MLA_HEADS, axis=1)
    sin = jnp.concatenate([st_ref[...]] * MLA_HEADS, axis=1)
    qm_ref[...] = (_bdot(qn, wq_ref[...]) * cos + _bdot(qn, wqr_ref[...]) * sin).astype(qm_ref.dtype)
    km_ref[...] = (_bdot(kvn, wk_ref[...]) + _bdot(kr_ref[...], e_ref[...])).astype(km_ref.dtype)
    vm_ref[...] = _bdot(kvn, wv_ref[...]).astype(vm_ref.dtype)


def _mla_prep(cq, ckv, kr, w, tabs, seq, tm):
    n = cq.shape[0]
    spt = seq // tm
    row = lambda i: (i, 0)
    full = lambda i: (0, 0)
    tab = lambda i: (i % spt, 0)
    hw = MLA_HEADS * MLA_SLOT
    return pl.pallas_call(
        _mla_prep_kernel,
        grid=(n // tm,),
        in_specs=[pl.BlockSpec((tm, cq.shape[1]), row), pl.BlockSpec((tm, ckv.shape[1]), row),
                  pl.BlockSpec((tm, 128), row),
                  pl.BlockSpec((1, cq.shape[1]), full), pl.BlockSpec((1, ckv.shape[1]), full),
                  pl.BlockSpec(w["wq"].shape, full), pl.BlockSpec(w["wqr"].shape, full),
                  pl.BlockSpec(w["wk"].shape, full), pl.BlockSpec(w["e"].shape, full),
                  pl.BlockSpec(w["wv"].shape, full),
                  pl.BlockSpec((tm, 128), tab), pl.BlockSpec((tm, 128), tab)],
        out_specs=[pl.BlockSpec((tm, hw), row), pl.BlockSpec((tm, hw), row),
                   pl.BlockSpec((tm, MLA_HEADS * MLA_V), row)],
        out_shape=[jax.ShapeDtypeStruct((n, hw), BF16), jax.ShapeDtypeStruct((n, hw), BF16),
                   jax.ShapeDtypeStruct((n, MLA_HEADS * MLA_V), BF16)],
        compiler_params=_params("parallel"),
        name="mla_prep",
    )(cq, ckv, kr, w["qn"], w["kvn"], w["wq"], w["wqr"], w["wk"], w["e"], w["wv"],
      tabs["mla_cos"], tabs["mla_sin"])


def _flash_chains(qs, k_ref, vt_ref, k_lanes, seq, tk, with_ones, kinds, ones_row):
    n_rows = [v.shape[0] for v in jax.eval_shape(with_ones, jax.ShapeDtypeStruct(vt_ref.shape[1:], vt_ref.dtype))]

    def step(kt, carry):
        off = pl.multiple_of(kt * tk, tk)
        variants = with_ones(vt_ref[kt])
        out = []
        scores = [lax.dot_general(k_ref[pl.ds(off, tk), lanes[0]:lanes[1]], q, _NT, preferred_element_type=F32)
                  for q, lanes in zip(qs, k_lanes)]
        for s, kind, (m, acc) in zip(scores, kinds, carry):
            m_new = jnp.maximum(m, jnp.max(s, axis=0, keepdims=True))
            alpha = jnp.exp2(m - m_new)
            p = jnp.exp2(s - m_new).astype(BF16)
            acc = alpha * acc + _bdot(variants[kind], p)
            out.append((m_new, acc))
        return tuple(out)

    init = tuple((jnp.full((1, q.shape[0]), NEG_INF, F32), jnp.zeros((n_rows[kind], q.shape[0]), F32))
                 for q, kind in zip(qs, kinds))
    final = lax.fori_loop(0, seq // tk, step, init)
    return [(acc, acc[row:row + 1, :]) for (_, acc), row in zip(final, ones_row)]


def _values_transposed(v, batch, seq, tk):
    groups = v.shape[1] // 128
    v5 = v.reshape(batch, seq // tk, tk, groups, 128)
    return jnp.transpose(v5, (0, 3, 1, 4, 2))


def _mla_kernel(q_ref, k_ref, vt_ref, o_ref, *, seq, tk):
    rows = q_ref.shape[0] // FLASH_Q_BLOCKS
    lanes = [(hh * MLA_SLOT, (hh + 1) * MLA_SLOT) for hh in range(2)]
    qs = [q_ref[r * rows:(r + 1) * rows, lo:hi] for r in range(FLASH_Q_BLOCKS) for lo, hi in lanes]

    def with_ones(vt):
        vrow = lax.broadcasted_iota(jnp.int32, vt.shape, 0)
        one = jnp.ones_like(vt)
        return [jnp.where(vrow < MLA_V, vt, one), jnp.where(vrow < MLA_V, one, vt)]

    res = _flash_chains(qs, k_ref, vt_ref, lanes * FLASH_Q_BLOCKS, seq, tk, with_ones,
                        kinds=[0, 1] * FLASH_Q_BLOCKS, ones_row=[MLA_V, 0] * FLASH_Q_BLOCKS)
    outs = [acc / l for acc, l in res]
    row = lax.broadcasted_iota(jnp.int32, outs[0].shape, 0)
    for r in range(FLASH_Q_BLOCKS):
        o_t = jnp.where(row < MLA_V, outs[2 * r], outs[2 * r + 1])
        o_ref[r * rows:(r + 1) * rows, :] = jnp.transpose(o_t).astype(o_ref.dtype)


def _mla(qm, km, vm, batch, seq, tq, tk):
    n = qm.shape[0]
    nq = seq // tq
    pairs = MLA_HEADS // 2
    return pl.pallas_call(
        functools.partial(_mla_kernel, seq=seq, tk=tk),
        grid=(batch, pairs, nq),
        in_specs=[pl.BlockSpec((tq, 2 * MLA_SLOT), lambda b, h, i: (b * nq + i, h)),
                  pl.BlockSpec((seq, 2 * MLA_SLOT), lambda b, h, i: (b, h)),
                  pl.BlockSpec((None, None, seq // tk, 2 * MLA_V, tk), lambda b, h, i: (b, h, 0, 0, 0))],
        out_specs=pl.BlockSpec((tq, 2 * MLA_V), lambda b, h, i: (b * nq + i, h)),
        out_shape=jax.ShapeDtypeStruct((n, MLA_HEADS * MLA_V), BF16),
        compiler_params=_params("parallel", "parallel", "parallel"),
        name="mla",
    )(qm, km, _values_transposed(vm, batch, seq, tk))


def _diff_kernel(q_ref, k_ref, vt_ref, lq1_ref, lk1_ref, lq2_ref, lk2_ref, sub_ref, o_ref, *,
                 seq, tk, lambda_init):
    rows = q_ref.shape[0] // FLASH_Q_BLOCKS
    lane = lax.broadcasted_iota(jnp.int32, (rows, 2 * DIFF_QK), 1)
    qs = []
    for r in range(FLASH_Q_BLOCKS):
        q = q_ref[r * rows:(r + 1) * rows, :]
        zero = jnp.zeros_like(q)
        qs += [jnp.where(lane < DIFF_QK, q, zero), jnp.where(lane >= DIFF_QK, q, zero)]
    def with_ones(vt):
        return [jnp.concatenate([vt, jnp.ones((16, vt.shape[1]), vt.dtype)], axis=0)]

    res = _flash_chains(qs, k_ref, vt_ref, [(0, 2 * DIFF_QK)] * len(qs), seq, tk, with_ones,
                        kinds=[0] * len(qs), ones_row=[DIFF_V] * len(qs))
    lam = (jnp.exp(jnp.sum(lq1_ref[...] * lk1_ref[...], axis=-1, keepdims=True))
           - jnp.exp(jnp.sum(lq2_ref[...] * lk2_ref[...], axis=-1, keepdims=True)) + lambda_init)
    for r in range(FLASH_Q_BLOCKS):
        (acc1, l1), (acc2, l2) = res[2 * r], res[2 * r + 1]
        o = jnp.transpose(acc1[:DIFF_V] / l1 - lam * (acc2[:DIFF_V] / l2))
        o = o * lax.rsqrt(jnp.mean(o * o, axis=-1, keepdims=True) + LN_EPS) * sub_ref[...]
        o_ref[r * rows:(r + 1) * rows, :] = (o * (1.0 - lambda_init)).astype(o_ref.dtype)


def _diff(qc, kc, vc, lams, subln, lambda_init, batch, seq, tq, tk):
    n = qc.shape[0]
    nq = seq // tq
    vec = lambda b, h, i: (0, 0)
    return pl.pallas_call(
        functools.partial(_diff_kernel, seq=seq, tk=tk, lambda_init=lambda_init),
        grid=(batch, DIFF_HEADS, nq),
        in_specs=[pl.BlockSpec((tq, 2 * DIFF_QK), lambda b, h, i: (b * nq + i, h)),
                  pl.BlockSpec((seq, 2 * DIFF_QK), lambda b, h, i: (b, h)),
                  pl.BlockSpec((None, None, seq // tk, DIFF_V, tk), lambda b, h, i: (b, h, 0, 0, 0)),
                  pl.BlockSpec((1, DIFF_QK), vec), pl.BlockSpec((1, DIFF_QK), vec),
                  pl.BlockSpec((1, DIFF_QK), vec), pl.BlockSpec((1, DIFF_QK), vec),
                  pl.BlockSpec((1, DIFF_V), vec)],
        out_specs=pl.BlockSpec((tq, DIFF_V), lambda b, h, i: (b * nq + i, h)),
        out_shape=jax.ShapeDtypeStruct((n, DIFF_HEADS * DIFF_V), BF16),
        compiler_params=_params("parallel", "parallel", "parallel"),
        name="diff",
    )(qc, kc, _values_transposed(vc, batch, seq, tk), *lams, subln)


def _nat_row_start(r, rows):
    return jnp.clip(r - NAT_KR // 2, 0, rows - NAT_KR)


def _nat_kernel(q_ref, k_ref, v_ref, bias_ref, o_ref, *, rows, rows_per_step):
    heads = [slice(h * HEAD_DIM, (h + 1) * HEAD_DIM) for h in range(NAT_HEADS)]
    values, logits = [], []
    for rr in range(rows_per_step):
        r = pl.program_id(1) * rows_per_step + rr
        r0 = _nat_row_start(r, rows)
        band = r0 - r + (NAT_KR - 1)
        off = pl.multiple_of(r0 * GRID_W, GRID_W)
        ks = k_ref[pl.ds(off, NAT_KR * GRID_W), :]
        values.append(v_ref[pl.ds(off, NAT_KR * GRID_W), :])
        q = q_ref[rr * GRID_W:(rr + 1) * GRID_W, :]
        logits.append([lax.dot_general(q[:, sl], ks[:, sl], _NT, preferred_element_type=F32) + bias_ref[band, h]
                       for h, sl in enumerate(heads)])
    probs = []
    for row_logits in logits:
        row_probs = []
        for s in row_logits:
            p = jnp.exp(s - jnp.max(s, axis=-1, keepdims=True))
            row_probs.append((p.astype(BF16), jnp.sum(p, axis=-1, keepdims=True)))
        probs.append(row_probs)
    for rr, (row_probs, vs) in enumerate(zip(probs, values)):
        outs = [_bdot(p, vs[:, sl]) / denom for (p, denom), sl in zip(row_probs, heads)]
        o_ref[rr * GRID_W:(rr + 1) * GRID_W, :] = jnp.concatenate(outs, axis=1).astype(o_ref.dtype)


def _nat_bias_table(rpb):
    heads, n_row_off, n_col_off = rpb.shape
    cols = np.arange(GRID_W)
    col_start = np.clip(cols - NAT_KC // 2, 0, GRID_W - NAT_KC)
    inside = (cols[None, :] >= col_start[:, None]) & (cols[None, :] < col_start[:, None] + NAT_KC)
    period = 2 * GRID_W
    lead = GRID_W - NAT_KC
    sig = jnp.pad(rpb.astype(F32), ((0, 0), (0, 0), (lead, period - n_col_off - lead)))
    flat = jnp.tile(sig, (1, 1, GRID_W))[..., :GRID_W * (period - 1)]
    toep = flat.reshape(heads, n_row_off, GRID_W, period - 1)[..., GRID_W - 1:]
    toep = jnp.where(inside[None, None], toep, NEG_INF)
    bands = [jnp.transpose(toep[:, d:d + NAT_KR], (0, 2, 1, 3)).reshape(heads, GRID_W, NAT_KR * GRID_W)
             for d in range(NAT_KR)]
    return jnp.stack(bands, axis=0)


def _nat(qd, kd, vd, rpb, batch, seq):
    n = qd.shape[0]
    rows = seq // GRID_W
    assert rows >= NAT_KR
    bias = _nat_bias_table(rpb)
    rps = NAT_ROWS_PER_STEP
    assert rows % rps == 0
    steps = rows // rps
    once = pl.Buffered(1)
    return pl.pallas_call(
        functools.partial(_nat_kernel, rows=rows, rows_per_step=rps),
        grid=(batch, steps),
        in_specs=[pl.BlockSpec((rps * GRID_W, 512), lambda b, r: (b * steps + r, 0)),
                  pl.BlockSpec((seq, 512), lambda b, r: (b, 0), pipeline_mode=once),
                  pl.BlockSpec((seq, 512), lambda b, r: (b, 0), pipeline_mode=once),
                  pl.BlockSpec(bias.shape, lambda b, r: (0, 0, 0, 0), pipeline_mode=once)],
        out_specs=pl.BlockSpec((rps * GRID_W, 512), lambda b, r: (b * steps + r, 0)),
        out_shape=jax.ShapeDtypeStruct((n, 512), BF16),
        compiler_params=_params("parallel", "arbitrary"),
        name="nat",
    )(qd, kd, vd, bias)


def _merge_kernel(x_ref, oa_ref, ob_ref, oc_ref, od_ref, sh_ref, sc_ref, gm_ref, wg_ref, wb_ref, wo_ref,
                  g_ref, b_ref, o_ref, *, alpha):
    x = x_ref[...]
    ub = (_norm_rows(x) * (1.0 + sc_ref[...]) + sh_ref[...]).astype(BF16)
    merged = None
    for n, br_ref in enumerate((oa_ref, ob_ref, oc_ref, od_ref)):
        term = _sigmoid(_bdot(ub, wg_ref[n])) * _bdot(br_ref[...], wb_ref[n])
        merged = term if merged is None else merged + term
    y = _bdot(merged.astype(BF16), wo_ref[...])
    z = alpha * x + gm_ref[...] * y
    o_ref[...] = _norm_rows(z) * g_ref[...] + b_ref[...]


def _merge(x2, branches, shift, scale, gate, wg, wb, wo, ln_g, ln_b, alpha, seq, tm):
    n, d = x2.shape
    spt = seq // tm
    row = lambda i: (i, 0)
    mod = lambda i: (i // spt, 0, 0)
    once = pl.Buffered(1)
    return pl.pallas_call(
        functools.partial(_merge_kernel, alpha=alpha),
        grid=(n // tm,),
        in_specs=[pl.BlockSpec((tm, d), row)] + [pl.BlockSpec((tm, 512), row)] * 4
                 + [pl.BlockSpec((None, 1, d), mod)] * 3
                 + [pl.BlockSpec(wg.shape, lambda i: (0, 0, 0), pipeline_mode=once),
                    pl.BlockSpec(wb.shape, lambda i: (0, 0, 0), pipeline_mode=once),
                    pl.BlockSpec(wo.shape, lambda i: (0, 0), pipeline_mode=once),
                    pl.BlockSpec((1, d), lambda i: (0, 0)), pl.BlockSpec((1, d), lambda i: (0, 0))],
        out_specs=pl.BlockSpec((tm, d), row),
        out_shape=jax.ShapeDtypeStruct((n, d), F32),
        compiler_params=_params("parallel"),
        name="merge",
    )(x2, *branches, shift, scale, gate, wg, wb, wo, ln_g, ln_b)


def _peer_candidate_blocks(k):
    blocks = []
    for i in range(k):
        need = min(k, (k + 1) // (i + 1))
        if need == 1:
            assert (k - i) % 8 == 0
            blocks.append((i, 1))
            break
        blocks.append((i, -(-need // 8) * 8))
    return tuple(blocks)


_PEER_CAND_BLOCKS = _peer_candidate_blocks(PEER_TOPK)
_PEER_CAND_ROWS = sum((PEER_TOPK - i) if w == 1 else w for i, w in _PEER_CAND_BLOCKS)


def _take_top(w, count, on_max, with_rank=False):
    rank = jnp.full(w.shape, float(count), F32) if with_rank else None
    for r in range(count):
        m = jnp.max(w, axis=0, keepdims=True)
        on_max(r, m)
        if r + 1 < count or with_rank:
            hit = w == m
            if with_rank:
                rank = jnp.where(hit, float(r), rank)
            if r + 1 < count:
                w = jnp.where(hit, -jnp.inf, w)
    return rank


def _peer_kernel(x_ref, sh_ref, sc_ref, gf_ref, g_ref, b_ref, wqt_ref, keys_ref, u_ref, vt_ref, o_ref,
                 ub_scr, q_scr, cnt_scr, rank_scr, p1_scr, p2_scr, top_scr, cand_scr, w_scr, acc_scr, *,
                 alpha, ipc, sub):
    c = pl.program_id(1)
    nk = PEER_NKEYS
    k = PEER_TOPK

    @pl.when(c == 0)
    def _scores_and_thresholds():
        u = _norm_rows(x_ref[...]) * (1.0 + sc_ref[...]) + sh_ref[...]
        ub_scr[...] = jnp.transpose(u).astype(BF16)
        q_scr[...] = _bdot(wqt_ref[...], ub_scr[...]).astype(BF16)

        def head(h, carry):
            halves = []
            for p in range(2):
                idx = 2 * h + p
                rows = pl.multiple_of(idx * nk, nk)
                s = _bdot(keys_ref[idx], q_scr[pl.ds(rows, nk), :])
                halves.append(s)

                def keep(r, m, p=p):
                    top_scr[p, r:r + 1, :] = m

                rank = _take_top(s, k + 1, keep, with_rank=(p == 1))
            s1, s2 = halves
            a = top_scr[0, :k, :]
            b = top_scr[1, :k, :]
            row = 0
            for i, width in _PEER_CAND_BLOCKS:
                if width == 1:
                    cand_scr[row:row + k - i, :] = a[i:k, :] + b[0:1, :]
                    row += k - i
                else:
                    cand_scr[row:row + width, :] = a[i:i + 1, :] + b[0:width, :]
                    row += width
            best = a[0:1, :] + b[0:1, :]
            stats = {"z": jnp.zeros_like(best)}

            def tally(r, m):
                if r < k:
                    stats["z"] = stats["z"] + jnp.exp(m - best)
                if r == k - 1:
                    stats["last"] = m
                if r == k:
                    stats["next"] = m

            _take_top(cand_scr[...], k + 1, tally)
            runner_up = jnp.maximum(stats["next"], jnp.maximum(top_scr[0, k:k + 1, :] + b[0:1, :],
                                                               a[0:1, :] + top_scr[1, k:k + 1, :]))
            bound = 0.5 * (stats["last"] + runner_up) - s1
            count = jnp.zeros_like(s1)
            for qq in range(k):
                count = jnp.where(b[qq:qq + 1, :] >= bound, float(qq + 1), count)
            cnt_scr[h] = count
            rank_scr[h] = rank.astype(BF16)
            p1_scr[h] = jnp.exp(s1 - a[0:1, :])
            p2_scr[h] = (jnp.exp(s2 - b[0:1, :]) / stats["z"]).astype(BF16)
            return carry

        lax.fori_loop(0, PEER_HEADS, head, 0)
        acc_scr[...] = jnp.zeros_like(acc_scr)

    ips = sub // nk
    n_sub = ipc // ips

    def hidden(sb):
        return _bdot(u_ref[sb * sub:(sb + 1) * sub, :], ub_scr[...])

    partial_out = None
    hids = [hidden(sb) for sb in range(n_sub)]
    for sb in range(n_sub):
        rows = slice(sb * sub, (sb + 1) * sub)
        hid = hids[sb]
        for il in range(ips):
            i = c * ipc + sb * ips + il
            gate = None
            for h in range(PEER_HEADS):
                cnt_row = cnt_scr[h, pl.ds(i, 1), :].astype(BF16)
                p1_row = p1_scr[h, pl.ds(i, 1), :].astype(BF16)
                chosen = rank_scr[h] < cnt_row
                term = jnp.where(chosen, p2_scr[h], 0.0) * p1_row
                gate = term if gate is None else gate + term
            hb = hid[il * nk:(il + 1) * nk, :]
            act = 0.5 * hb * (1.0 + lax.erf(hb * (2.0 ** -0.5)))
            w_scr[sb * sub + il * nk:sb * sub + (il + 1) * nk, :] = gate * act.astype(BF16)
        out = _bdot(vt_ref[:, rows], w_scr[rows, :])
        partial_out = out if partial_out is None else partial_out + out
    acc_scr[...] += partial_out

    @pl.when(c == pl.num_programs(1) - 1)
    def _finish():
        y = jnp.transpose(acc_scr[...])
        z = alpha * x_ref[...] + gf_ref[...] * y
        o_ref[...] = _norm_rows(z) * g_ref[...] + b_ref[...]


def _peer(x2, shift, scale, gate, ln_g, ln_b, wqt, keys, u_tab, vt_tab, alpha, seq, tt):
    n, d = x2.shape
    chunk, sub = PEER_CHUNK, PEER_SUB
    assert u_tab.shape[0] % chunk == 0 and chunk % sub == 0 and sub % PEER_NKEYS == 0
    spt = seq // tt
    experts = u_tab.shape[0]
    ipc = chunk // PEER_NKEYS
    nhp = 2 * PEER_HEADS
    row = lambda i, c: (i, 0)
    mod = lambda i, c: (i // spt, 0, 0)
    return pl.pallas_call(
        functools.partial(_peer_kernel, alpha=alpha, ipc=ipc, sub=sub),
        grid=(n // tt, experts // chunk),
        in_specs=[pl.BlockSpec((tt, d), row)] + [pl.BlockSpec((None, 1, d), mod)] * 3
                 + [pl.BlockSpec((1, d), lambda i, c: (0, 0)), pl.BlockSpec((1, d), lambda i, c: (0, 0)),
                    pl.BlockSpec(wqt.shape, lambda i, c: (0, 0), pipeline_mode=pl.Buffered(1)),
                    pl.BlockSpec(keys.shape, lambda i, c: (0, 0, 0), pipeline_mode=pl.Buffered(1)),
                    pl.BlockSpec((chunk, d), lambda i, c: (c, 0)),
                    pl.BlockSpec((d, chunk), lambda i, c: (0, c))],
        out_specs=pl.BlockSpec((tt, d), row),
        out_shape=jax.ShapeDtypeStruct((n, d), F32),
        scratch_shapes=[pltpu.VMEM((d, tt), BF16),
                        pltpu.VMEM((nhp * PEER_NKEYS, tt), BF16),
                        pltpu.VMEM((PEER_HEADS, PEER_NKEYS, tt), F32),
                        pltpu.VMEM((PEER_HEADS, PEER_NKEYS, tt), BF16),
                        pltpu.VMEM((PEER_HEADS, PEER_NKEYS, tt), F32),
                        pltpu.VMEM((PEER_HEADS, PEER_NKEYS, tt), BF16),
                        pltpu.VMEM((2, PEER_TOPK + 8, tt), F32),
                        pltpu.VMEM((_PEER_CAND_ROWS, tt), F32),
                        pltpu.VMEM((chunk, tt), BF16),
                        pltpu.VMEM((d, tt), F32)],
        compiler_params=_params("parallel", "arbitrary"),
        name="peer",
    )(x2, shift, scale, gate, ln_g, ln_b, wqt, keys, u_tab, vt_tab)


def _rotate_half_cols(w, dim):
    rows, cols = w.shape
    w3 = w.reshape(rows, cols // dim, dim)
    return jnp.concatenate([-w3[..., dim // 2:], w3[..., :dim // 2]], axis=-1).reshape(rows, cols)


def _pad_cols(w, width):
    return jnp.pad(w, ((0, 0), (0, width - w.shape[1])))


def _in_proj_weights(w_in):
    widths = (512, 128, 128, 384, 256, 32, 512, 512, 512, 512, 512, 512)
    names = ("qa", "ka", "va", "cq", "ckv", "kr", "qc", "kc", "vc", "qd", "kd", "vd")
    parts, off = {}, 0
    for name, width in zip(names, widths):
        parts[name] = w_in[:, off:off + width]
        off += width
    for name, extra in (("qa", 1.0), ("qc", LOG2_E), ("qd", 1.0)):
        parts[name] = parts[name] * (HEAD_DIM ** -0.5 * extra)
    rot_dim = {"qa": HEAD_DIM, "ka": HEAD_DIM, "qc": DIFF_QK, "kc": DIFF_QK, "kr": MLA_ROPE}
    cols = [_pad_cols(parts[name], width) for name, width in _MAIN_GROUPS]
    cols += [_pad_cols(_rotate_half_cols(parts[name], rot_dim[name]), width) for name, width in _ROT_GROUPS]
    return jnp.concatenate(cols, axis=1).astype(BF16)


def _mla_weights(q_norm, q_up, kv_norm, kv_up):
    qr, kvr = q_up.shape[0], kv_up.shape[0]
    qh = q_up.reshape(qr, MLA_HEADS, MLA_NOPE + MLA_ROPE) * ((MLA_NOPE + MLA_ROPE) ** -0.5 * LOG2_E)
    pad = MLA_SLOT - MLA_NOPE - MLA_ROPE
    wq = jnp.pad(qh, ((0, 0), (0, 0), (0, pad))).reshape(qr, MLA_HEADS * MLA_SLOT)
    rot = _rotate_half_cols(qh[..., MLA_NOPE:].reshape(qr, MLA_HEADS * MLA_ROPE), MLA_ROPE)
    rot = rot.reshape(qr, MLA_HEADS, MLA_ROPE)
    wqr = jnp.pad(rot, ((0, 0), (0, 0), (MLA_NOPE, pad))).reshape(qr, MLA_HEADS * MLA_SLOT)
    kvh = kv_up.reshape(kvr, MLA_HEADS, MLA_NOPE + MLA_V)
    wk = jnp.pad(kvh[..., :MLA_NOPE], ((0, 0), (0, 0), (0, MLA_SLOT - MLA_NOPE))).reshape(kvr, MLA_HEADS * MLA_SLOT)
    wv = kvh[..., MLA_NOPE:].reshape(kvr, MLA_HEADS * MLA_V)
    place = np.zeros((128, MLA_HEADS, MLA_SLOT), np.float32)
    for r in range(MLA_ROPE):
        place[r, :, MLA_NOPE + r] = 1.0
    return {"qn": q_norm.reshape(1, qr), "kvn": kv_norm.reshape(1, kvr),
            "wq": wq.astype(BF16), "wqr": wqr.astype(BF16), "wk": wk.astype(BF16), "wv": wv.astype(BF16),
            "e": jnp.asarray(place.reshape(128, MLA_HEADS * MLA_SLOT), BF16)}


def _rope_tables(seq):
    def base(dim):
        inv = ROPE_THETA ** (-jnp.arange(0, dim, 2, dtype=F32) / dim)
        ang = jnp.arange(seq, dtype=F32)[:, None] * inv[None, :]
        cos, sin = jnp.cos(ang), jnp.sin(ang)
        return jnp.concatenate([cos, cos], axis=1), jnp.concatenate([sin, sin], axis=1)

    cos64, sin64 = base(HEAD_DIM)
    cos32, sin32 = base(MLA_ROPE)
    ones = jnp.ones((seq, MLA_NOPE), F32)
    zeros = jnp.zeros((seq, MLA_NOPE), F32)
    tail = jnp.zeros((seq, MLA_SLOT - MLA_NOPE - MLA_ROPE), F32)
    return {"cos64": jnp.concatenate([cos64, cos64], axis=1), "sin64": jnp.concatenate([sin64, sin64], axis=1),
            "cos32": _pad_cols(cos32, 128), "sin32": _pad_cols(sin32, 128),
            "mla_cos": jnp.concatenate([ones, cos32, tail], axis=1),
            "mla_sin": jnp.concatenate([zeros, sin32, tail], axis=1)}


def _tile(total, want):
    t = min(total, want)
    assert total % t == 0
    return t


def kernel(x, c, ada_w, ada_b, w_in, swa_sink, mla_q_norm, mla_q_up, mla_kv_norm, mla_kv_up, diff_lambda_q1, diff_lambda_k1, diff_lambda_q2, diff_lambda_k2, diff_subln, nat_rpb, w_gate, w_branch, w_out, ln1_g, ln1_b, peer_wq, peer_keys, peer_u, peer_v, ln2_g, ln2_b):
    batch, seq, d = x.shape
    depth = ada_w.shape[0]
    alpha = (2 * depth) ** 0.25
    n = batch * seq
    tabs = _rope_tables(seq)
    mod = _ada(c, ada_w, ada_b)
    tm = _tile(seq, 512)
    x2 = x.reshape(n, d)
    for l in range(depth):
        sh_mix, sc_mix, g_mix, sh_ffn, sc_ffn, g_ffn = (mod[l, :, j] for j in range(6))
        (qa, ka, va, cq, ckv, kr, qc, kc, vc, qd, kd, vd) = _in_proj(
            x2, sh_mix, sc_mix, _in_proj_weights(w_in[l]), tabs, seq, tm)
        o_a = _swa(qa, ka, va, swa_sink[l], batch, seq, _tile(seq, SWA_TQ))
        qm, km, vm = _mla_prep(cq, ckv, kr, _mla_weights(mla_q_norm[l], mla_q_up[l], mla_kv_norm[l], mla_kv_up[l]),
                               tabs, seq, tm)
        o_b = _mla(qm, km, vm, batch, seq, _tile(seq, FLASH_TQ), _tile(seq, FLASH_TK))
        lambda_init = 0.8 - 0.6 * math.exp(-0.3 * l)
        lams = tuple(v[l].reshape(1, DIFF_QK) for v in (diff_lambda_q1, diff_lambda_k1, diff_lambda_q2, diff_lambda_k2))
        o_c = _diff(qc, kc, vc, lams, diff_subln[l].reshape(1, DIFF_V), lambda_init, batch, seq,
                    _tile(seq, FLASH_TQ), _tile(seq, FLASH_TK))
        o_d = _nat(qd, kd, vd, nat_rpb[l], batch, seq)
        x2 = _merge(x2, (o_a, o_b, o_c, o_d), sh_mix, sc_mix, g_mix, w_gate[l].astype(BF16),
                    w_branch[l].astype(BF16), w_out[l].astype(BF16), ln1_g[l].reshape(1, d), ln1_b[l].reshape(1, d),
                    alpha, seq, tm)
        keys = peer_keys[l].reshape(2 * PEER_HEADS, PEER_NKEYS, PEER_DKEY // 2).astype(BF16)
        x2 = _peer(x2, sh_ffn, sc_ffn, g_ffn, ln2_g[l].reshape(1, d), ln2_b[l].reshape(1, d),
                   peer_wq[l].T.astype(BF16), keys, peer_u[l].astype(BF16), peer_v[l].T.astype(BF16),
                   alpha, seq, tm)
    return x2.reshape(batch, seq, d)
```

```python
import functools
import math

import jax
import jax.numpy as jnp
import numpy as np
from jax import lax
from jax.experimental import pallas as pl
from jax.experimental.pallas import tpu as pltpu

F32 = jnp.float32
BF16 = jnp.bfloat16

GRID_W = 64
ROPE_THETA = 10000.0
HEAD_DIM = 64
LN_EPS = 1e-5
NEG_INF = -1e30
LOG2_E = 1.4426950408889634

SWA_HEADS = 8
SWA_KV_HEADS = 2
SWA_WINDOW = 128
MLA_HEADS = 8
MLA_NOPE = 64
MLA_ROPE = 32
MLA_V = 64
MLA_SLOT = 128
DIFF_HEADS = 4
DIFF_QK = 64
DIFF_V = 128
NAT_HEADS = 8
NAT_KR = 8
NAT_KC = 16
PEER_HEADS = 8
PEER_NKEYS = 128
PEER_DKEY = 256
PEER_TOPK = 16

VMEM_LIMIT_BYTES = 56 * 1024 * 1024
FLASH_TQ = 2048
FLASH_Q_BLOCKS = 4
FLASH_TK = 1024
SWA_TQ = 256
NAT_ROWS_PER_STEP = 4
PEER_CHUNK = 2048
PEER_SUB = 512

_MAIN_GROUPS = (("qa", 512), ("ka", 128), ("va", 128), ("cq", 384), ("ckv", 256), ("qc", 512), ("kc", 512),
                ("vc", 512), ("qd", 512), ("kd", 512), ("vd", 512), ("kr", 128))
_ROT_GROUPS = (("qa", 512), ("ka", 128), ("qc", 512), ("kc", 512), ("kr", 128))


def _offsets(groups, base=0):
    out, off = {}, base
    for name, width in groups:
        out[name] = (off, off + width)
        off += width
    return out, off


_MAIN_OFF, _MAIN_END = _offsets(_MAIN_GROUPS)
_ROT_OFF, _W_ALL_COLS = _offsets(_ROT_GROUPS, _MAIN_END)

_NT = (((1,), (1,)), ((), ()))


def _params(*sem):
    return pltpu.CompilerParams(dimension_semantics=sem, vmem_limit_bytes=VMEM_LIMIT_BYTES)


def _norm_rows(x):
    mu = jnp.mean(x, axis=-1, keepdims=True)
    xc = x - mu
    var = jnp.mean(xc * xc, axis=-1, keepdims=True)
    return xc * lax.rsqrt(var + LN_EPS)


def _sigmoid(z):
    return 1.0 / (1.0 + jnp.exp(-z))


def _bdot(a, b):
    return jnp.dot(a, b, preferred_element_type=F32)


def _ada_kernel(c_ref, w_ref, b_ref, o_ref):
    c = c_ref[...]
    act = c * _sigmoid(c)
    o_ref[...] = jnp.dot(act, w_ref[...], preferred_element_type=F32,
                         precision=lax.Precision.HIGHEST) + b_ref[...]


def _ada(c, ada_w, ada_b):
    depth, d, six_d = ada_w.shape
    b = c.shape[0]
    rows = -(-b // 8) * 8
    c_pad = jnp.pad(c, ((0, rows - b), (0, 0)))
    out = pl.pallas_call(
        _ada_kernel,
        grid=(depth, six_d // d),
        in_specs=[pl.BlockSpec((rows, d), lambda l, j: (0, 0)),
                  pl.BlockSpec((None, d, d), lambda l, j: (l, 0, j)),
                  pl.BlockSpec((None, 1, d), lambda l, j: (l, 0, j))],
        out_specs=pl.BlockSpec((None, rows, d), lambda l, j: (l, 0, j)),
        out_shape=jax.ShapeDtypeStruct((depth, rows, six_d), F32),
        compiler_params=_params("parallel", "parallel"),
        name="ada",
    )(c_pad, ada_w, ada_b.reshape(depth, 1, six_d))
    return out[:, :b].reshape(depth, b, 6, 1, d)


def _in_proj_kernel(x_ref, sh_ref, sc_ref, w_ref, cos_ref, sin_ref, cos32_ref, sin32_ref,
                    qa_ref, ka_ref, va_ref, cq_ref, ckv_ref, kr_ref, qc_ref, kc_ref, vc_ref,
                    qd_ref, kd_ref, vd_ref):
    u = _norm_rows(x_ref[...]) * (1.0 + sc_ref[...]) + sh_ref[...]
    ub = u.astype(BF16)

    def proj(off):
        return _bdot(ub, w_ref[:, off[0]:off[1]])

    def roped(name, cos, sin):
        main, rot = proj(_MAIN_OFF[name]), proj(_ROT_OFF[name])
        reps = main.shape[1] // cos.shape[1]
        if reps > 1:
            cos = jnp.concatenate([cos] * reps, axis=1)
            sin = jnp.concatenate([sin] * reps, axis=1)
        return main * cos + rot * sin

    cos, sin = cos_ref[...], sin_ref[...]
    qa_ref[...] = roped("qa", cos, sin).astype(qa_ref.dtype)
    ka_ref[...] = roped("ka", cos, sin).astype(ka_ref.dtype)
    va_ref[...] = proj(_MAIN_OFF["va"]).astype(va_ref.dtype)
    cq_ref[...] = proj(_MAIN_OFF["cq"])
    ckv_ref[...] = proj(_MAIN_OFF["ckv"])
    kr_ref[...] = roped("kr", cos32_ref[...], sin32_ref[...]).astype(kr_ref.dtype)
    qc_ref[...] = roped("qc", cos, sin).astype(qc_ref.dtype)
    kc_ref[...] = roped("kc", cos, sin).astype(kc_ref.dtype)
    vc_ref[...] = proj(_MAIN_OFF["vc"]).astype(vc_ref.dtype)
    qd_ref[...] = proj(_MAIN_OFF["qd"]).astype(qd_ref.dtype)
    kd_ref[...] = proj(_MAIN_OFF["kd"]).astype(kd_ref.dtype)
    vd_ref[...] = proj(_MAIN_OFF["vd"]).astype(vd_ref.dtype)


def _in_proj(x2, shift, scale, w_all, tabs, seq, tm):
    n, d = x2.shape
    spt = seq // tm
    widths = dict(_MAIN_GROUPS)
    names = ("qa", "ka", "va", "cq", "ckv", "kr", "qc", "kc", "vc", "qd", "kd", "vd")
    dtypes = {k: BF16 for k in names}
    dtypes["cq"] = F32
    dtypes["ckv"] = F32
    row = lambda i: (i, 0)
    mod = lambda i: (i // spt, 0, 0)
    tab = lambda i: (i % spt, 0)
    return pl.pallas_call(
        _in_proj_kernel,
        grid=(n // tm,),
        in_specs=[pl.BlockSpec((tm, d), row),
                  pl.BlockSpec((None, 1, d), mod),
                  pl.BlockSpec((None, 1, d), mod),
                  pl.BlockSpec((d, _W_ALL_COLS), lambda i: (0, 0), pipeline_mode=pl.Buffered(1)),
                  pl.BlockSpec((tm, 128), tab), pl.BlockSpec((tm, 128), tab),
                  pl.BlockSpec((tm, 128), tab), pl.BlockSpec((tm, 128), tab)],
        out_specs=[pl.BlockSpec((tm, widths[k]), row) for k in names],
        out_shape=[jax.ShapeDtypeStruct((n, widths[k]), dtypes[k]) for k in names],
        compiler_params=_params("parallel"),
        name="in_proj",
    )(x2, shift, scale, w_all, tabs["cos64"], tabs["sin64"], tabs["cos32"], tabs["sin32"])


def _swa_kernel(q_ref, k_ref, v_ref, sink_ref, o_ref, *, tq, seq):
    i = pl.program_id(1)
    span = tq + 2 * SWA_WINDOW
    start = pl.multiple_of(jnp.clip(i * tq - SWA_WINDOW, 0, seq - span), SWA_WINDOW)
    ks = k_ref[pl.ds(start, span), :]
    vs = v_ref[pl.ds(start, span), :]
    qpos = i * tq + lax.broadcasted_iota(jnp.int32, (tq, span), 0)
    kpos = start + lax.broadcasted_iota(jnp.int32, (tq, span), 1)
    valid = jnp.abs(qpos - kpos) <= SWA_WINDOW
    q = q_ref[...]
    grp = SWA_HEADS // SWA_KV_HEADS
    outs = []
    for h in range(SWA_HEADS):
        kv = h // grp
        qh = q[:, h * HEAD_DIM:(h + 1) * HEAD_DIM]
        kh = ks[:, kv * HEAD_DIM:(kv + 1) * HEAD_DIM]
        vh = vs[:, kv * HEAD_DIM:(kv + 1) * HEAD_DIM]
        s = lax.dot_general(qh, kh, _NT, preferred_element_type=F32)
        s = jnp.where(valid, s, NEG_INF)
        sink = sink_ref[h][:, :1]
        m = jnp.maximum(jnp.max(s, axis=-1, keepdims=True), sink)
        p = jnp.exp(s - m)
        denom = jnp.sum(p, axis=-1, keepdims=True) + jnp.exp(sink - m)
        outs.append(_bdot(p.astype(BF16), vh) / denom)
    o_ref[...] = jnp.concatenate(outs, axis=1).astype(o_ref.dtype)


def _swa(qa, ka, va, sink, batch, seq, tq):
    n = qa.shape[0]
    nq = seq // tq
    sink_b = jnp.broadcast_to(sink.astype(F32)[:, None, None], (SWA_HEADS, 1, 128))
    return pl.pallas_call(
        functools.partial(_swa_kernel, tq=tq, seq=seq),
        grid=(batch, nq),
        in_specs=[pl.BlockSpec((tq, 512), lambda b, i: (b * nq + i, 0)),
                  pl.BlockSpec((seq, ka.shape[1]), lambda b, i: (b, 0)),
                  pl.BlockSpec((seq, va.shape[1]), lambda b, i: (b, 0)),
                  pl.BlockSpec((SWA_HEADS, 1, 128), lambda b, i: (0, 0, 0))],
        out_specs=pl.BlockSpec((tq, 512), lambda b, i: (b * nq + i, 0)),
        out_shape=jax.ShapeDtypeStruct((n, 512), BF16),
        compiler_params=_params("parallel", "parallel"),
        name="swa",
    )(qa, ka, va, sink_b)


def _mla_prep_kernel(cq_ref, ckv_ref, kr_ref, qn_ref, kvn_ref, wq_ref, wqr_ref, wk_ref, e_ref, wv_ref,
                     ct_ref, st_ref, qm_ref, km_ref, vm_ref):
    def rms(x, g):
        return (x * lax.rsqrt(jnp.mean(x * x, axis=-1, keepdims=True) + LN_EPS) * g).astype(BF16)

    qn = rms(cq_ref[...], qn_ref[...])
    kvn = rms(ckv_ref[...], kvn_ref[...])
    cos = jnp.concatenate([ct_ref[...]] * MLA_HEADS, axis=1)
    sin = jnp.concatenate([st_ref[...]] * MLA_HEADS, axis=1)
    qm_ref[...] = (_bdot(qn, wq_ref[...]) * cos + _bdot(qn, wqr_ref[...]) * sin).astype(qm_ref.dtype)
    km_ref[...] = (_bdot(kvn, wk_ref[...]) + _bdot(kr_ref[...], e_ref[...])).astype(km_ref.dtype)
    vm_ref[...] = _bdot(kvn, wv_ref[...]).astype(vm_ref.dtype)


def _mla_prep(cq, ckv, kr, w, tabs, seq, tm):
    n = cq.shape[0]
    spt = seq // tm
    row = lambda i: (i, 0)
    full = lambda i: (0, 0)
    tab = lambda i: (i % spt, 0)
    hw = MLA_HEADS * MLA_SLOT
    return pl.pallas_call(
        _mla_prep_kernel,
        grid=(n // tm,),
        in_specs=[pl.BlockSpec((tm, cq.shape[1]), row), pl.BlockSpec((tm, ckv.shape[1]), row),
                  pl.BlockSpec((tm, 128), row),
                  pl.BlockSpec((1, cq.shape[1]), full), pl.BlockSpec((1, ckv.shape[1]), full),
                  pl.BlockSpec(w["wq"].shape, full), pl.BlockSpec(w["wqr"].shape, full),
                  pl.BlockSpec(w["wk"].shape, full), pl.BlockSpec(w["e"].shape, full),
                  pl.BlockSpec(w["wv"].shape, full),
                  pl.BlockSpec((tm, 128), tab), pl.BlockSpec((tm, 128), tab)],
        out_specs=[pl.BlockSpec((tm, hw), row), pl.BlockSpec((tm, hw), row),
                   pl.BlockSpec((tm, MLA_HEADS * MLA_V), row)],
        out_shape=[jax.ShapeDtypeStruct((n, hw), BF16), jax.ShapeDtypeStruct((n, hw), BF16),
                   jax.ShapeDtypeStruct((n, MLA_HEADS * MLA_V), BF16)],
        compiler_params=_params("parallel"),
        name="mla_prep",
    )(cq, ckv, kr, w["qn"], w["kvn"], w["wq"], w["wqr"], w["wk"], w["e"], w["wv"],
      tabs["mla_cos"], tabs["mla_sin"])


def _flash_chains(qs, k_ref, vt_ref, k_lanes, seq, tk, with_ones, kinds, ones_row):
    n_rows = [v.shape[0] for v in jax.eval_shape(with_ones, jax.ShapeDtypeStruct(vt_ref.shape[1:], vt_ref.dtype))]

    def step(kt, carry):
        off = pl.multiple_of(kt * tk, tk)
        variants = with_ones(vt_ref[kt])
        out = []
        scores = [lax.dot_general(k_ref[pl.ds(off, tk), lanes[0]:lanes[1]], q, _NT, preferred_element_type=F32)
                  for q, lanes in zip(qs, k_lanes)]
        for s, kind, (m, acc) in zip(scores, kinds, carry):
            m_new = jnp.maximum(m, jnp.max(s, axis=0, keepdims=True))
            alpha = jnp.exp2(m - m_new)
            p = jnp.exp2(s - m_new).astype(BF16)
            acc = alpha * acc + _bdot(variants[kind], p)
            out.append((m_new, acc))
        return tuple(out)

    init = tuple((jnp.full((1, q.shape[0]), NEG_INF, F32), jnp.zeros((n_rows[kind], q.shape[0]), F32))
                 for q, kind in zip(qs, kinds))
    final = lax.fori_loop(0, seq // tk, step, init)
    return [(acc, acc[row:row + 1, :]) for (_, acc), row in zip(final, ones_row)]


def _values_transposed(v, batch, seq, tk):
    groups = v.shape[1] // 128
    v5 = v.reshape(batch, seq // tk, tk, groups, 128)
    return jnp.transpose(v5, (0, 3, 1, 4, 2))


def _mla_kernel(q_ref, k_ref, vt_ref, o_ref, *, seq, tk):
    rows = q_ref.shape[0] // FLASH_Q_BLOCKS
    lanes = [(hh * MLA_SLOT, (hh + 1) * MLA_SLOT) for hh in range(2)]
    qs = [q_ref[r * rows:(r + 1) * rows, lo:hi] for r in range(FLASH_Q_BLOCKS) for lo, hi in lanes]

    def with_ones(vt):
        vrow = lax.broadcasted_iota(jnp.int32, vt.shape, 0)
        one = jnp.ones_like(vt)
        return [jnp.where(vrow < MLA_V, vt, one), jnp.where(vrow < MLA_V, one, vt)]

    res = _flash_chains(qs, k_ref, vt_ref, lanes * FLASH_Q_BLOCKS, seq, tk, with_ones,
                        kinds=[0, 1] * FLASH_Q_BLOCKS, ones_row=[MLA_V, 0] * FLASH_Q_BLOCKS)
    outs = [acc / l for acc, l in res]
    row = lax.broadcasted_iota(jnp.int32, outs[0].shape, 0)
    for r in range(FLASH_Q_BLOCKS):
        o_t = jnp.where(row < MLA_V, outs[2 * r], outs[2 * r + 1])
        o_ref[r * rows:(r + 1) * rows, :] = jnp.transpose(o_t).astype(o_ref.dtype)


def _mla(qm, km, vm, batch, seq, tq, tk):
    n = qm.shape[0]
    nq = seq // tq
    pairs = MLA_HEADS // 2
    return pl.pallas_call(
        functools.partial(_mla_kernel, seq=seq, tk=tk),
        grid=(batch, pairs, nq),
        in_specs=[pl.BlockSpec((tq, 2 * MLA_SLOT), lambda b, h, i: (b * nq + i, h)),
                  pl.BlockSpec((seq, 2 * MLA_SLOT), lambda b, h, i: (b, h)),
                  pl.BlockSpec((None, None, seq // tk, 2 * MLA_V, tk), lambda b, h, i: (b, h, 0, 0, 0))],
        out_specs=pl.BlockSpec((tq, 2 * MLA_V), lambda b, h, i: (b * nq + i, h)),
        out_shape=jax.ShapeDtypeStruct((n, MLA_HEADS * MLA_V), BF16),
        compiler_params=_params("parallel", "parallel", "parallel"),
        name="mla",
    )(qm, km, _values_transposed(vm, batch, seq, tk))


def _diff_kernel(q_ref, k_ref, vt_ref, lq1_ref, lk1_ref, lq2_ref, lk2_ref, sub_ref, o_ref, *,
                 seq, tk, lambda_init):
    rows = q_ref.shape[0] // FLASH_Q_BLOCKS
    lane = lax.broadcasted_iota(jnp.int32, (rows, 2 * DIFF_QK), 1)
    qs = []
    for r in range(FLASH_Q_BLOCKS):
        q = q_ref[r * rows:(r + 1) * rows, :]
        zero = jnp.zeros_like(q)
        qs += [jnp.where(lane < DIFF_QK, q, zero), jnp.where(lane >= DIFF_QK, q, zero)]
    def with_ones(vt):
        return [jnp.concatenate([vt, jnp.ones((16, vt.shape[1]), vt.dtype)], axis=0)]

    res = _flash_chains(qs, k_ref, vt_ref, [(0, 2 * DIFF_QK)] * len(qs), seq, tk, with_ones,
                        kinds=[0] * len(qs), ones_row=[DIFF_V] * len(qs))
    lam = (jnp.exp(jnp.sum(lq1_ref[...] * lk1_ref[...], axis=-1, keepdims=True))
           - jnp.exp(jnp.sum(lq2_ref[...] * lk2_ref[...], axis=-1, keepdims=True)) + lambda_init)
    for r in range(FLASH_Q_BLOCKS):
        (acc1, l1), (acc2, l2) = res[2 * r], res[2 * r + 1]
        o = jnp.transpose(acc1[:DIFF_V] / l1 - lam * (acc2[:DIFF_V] / l2))
        o = o * lax.rsqrt(jnp.mean(o * o, axis=-1, keepdims=True) + LN_EPS) * sub_ref[...]
        o_ref[r * rows:(r + 1) * rows, :] = (o * (1.0 - lambda_init)).astype(o_ref.dtype)


def _diff(qc, kc, vc, lams, subln, lambda_init, batch, seq, tq, tk):
    n = qc.shape[0]
    nq = seq // tq
    vec = lambda b, h, i: (0, 0)
    return pl.pallas_call(
        functools.partial(_diff_kernel, seq=seq, tk=tk, lambda_init=lambda_init),
        grid=(batch, DIFF_HEADS, nq),
        in_specs=[pl.BlockSpec((tq, 2 * DIFF_QK), lambda b, h, i: (b * nq + i, h)),
                  pl.BlockSpec((seq, 2 * DIFF_QK), lambda b, h, i: (b, h)),
                  pl.BlockSpec((None, None, seq // tk, DIFF_V, tk), lambda b, h, i: (b, h, 0, 0, 0)),
                  pl.BlockSpec((1, DIFF_QK), vec), pl.BlockSpec((1, DIFF_QK), vec),
                  pl.BlockSpec((1, DIFF_QK), vec), pl.BlockSpec((1, DIFF_QK), vec),
                  pl.BlockSpec((1, DIFF_V), vec)],
        out_specs=pl.BlockSpec((tq, DIFF_V), lambda b, h, i: (b * nq + i, h)),
        out_shape=jax.ShapeDtypeStruct((n, DIFF_HEADS * DIFF_V), BF16),
        compiler_params=_params("parallel", "parallel", "parallel"),
        name="diff",
    )(qc, kc, _values_transposed(vc, batch, seq, tk), *lams, subln)


def _nat_row_start(r, rows):
    return jnp.clip(r - NAT_KR // 2, 0, rows - NAT_KR)


def _nat_kernel(q_ref, k_ref, v_ref, bias_ref, o_ref, *, rows, rows_per_step):
    pairs = [slice(pp * 2 * HEAD_DIM, (pp + 1) * 2 * HEAD_DIM) for pp in range(NAT_HEADS // 2)]
    lane = lax.broadcasted_iota(jnp.int32, (GRID_W, 2 * HEAD_DIM), 1)
    values, logits = [], []
    for rr in range(rows_per_step):
        r = pl.program_id(1) * rows_per_step + rr
        r0 = _nat_row_start(r, rows)
        band = r0 - r + (NAT_KR - 1)
        off = pl.multiple_of(r0 * GRID_W, GRID_W)
        ks = k_ref[pl.ds(off, NAT_KR * GRID_W), :]
        values.append(v_ref[pl.ds(off, NAT_KR * GRID_W), :])
        q = q_ref[rr * GRID_W:(rr + 1) * GRID_W, :]
        row_logits = []
        for pp, sl in enumerate(pairs):
            qp = q[:, sl]
            zero = jnp.zeros_like(qp)
            for half, qh in enumerate((jnp.where(lane < HEAD_DIM, qp, zero), jnp.where(lane < HEAD_DIM, zero, qp))):
                row_logits.append(lax.dot_general(qh, ks[:, sl], _NT, preferred_element_type=F32)
                                  + bias_ref[band, 2 * pp + half])
        logits.append(row_logits)
    probs = []
    for row_logits in logits:
        row_probs = []
        for s in row_logits:
            p = jnp.exp(s - jnp.max(s, axis=-1, keepdims=True))
            row_probs.append((p.astype(BF16), jnp.sum(p, axis=-1, keepdims=True)))
        probs.append(row_probs)
    for rr, (row_probs, vs) in enumerate(zip(probs, values)):
        outs = []
        for pp, sl in enumerate(pairs):
            (p0, d0), (p1, d1) = row_probs[2 * pp], row_probs[2 * pp + 1]
            outs.append(jnp.where(lane < HEAD_DIM, _bdot(p0, vs[:, sl]) / d0, _bdot(p1, vs[:, sl]) / d1))
        o_ref[rr * GRID_W:(rr + 1) * GRID_W, :] = jnp.concatenate(outs, axis=1).astype(o_ref.dtype)


def _nat_bias_table(rpb):
    heads, n_row_off, n_col_off = rpb.shape
    cols = np.arange(GRID_W)
    col_start = np.clip(cols - NAT_KC // 2, 0, GRID_W - NAT_KC)
    inside = (cols[None, :] >= col_start[:, None]) & (cols[None, :] < col_start[:, None] + NAT_KC)
    period = 2 * GRID_W
    lead = GRID_W - NAT_KC
    sig = jnp.pad(rpb.astype(F32), ((0, 0), (0, 0), (lead, period - n_col_off - lead)))
    flat = jnp.tile(sig, (1, 1, GRID_W))[..., :GRID_W * (period - 1)]
    toep = flat.reshape(heads, n_row_off, GRID_W, period - 1)[..., GRID_W - 1:]
    toep = jnp.where(inside[None, None], toep, NEG_INF)
    bands = [jnp.transpose(toep[:, d:d + NAT_KR], (0, 2, 1, 3)).reshape(heads, GRID_W, NAT_KR * GRID_W)
             for d in range(NAT_KR)]
    return jnp.stack(bands, axis=0)


def _nat(qd, kd, vd, rpb, batch, seq):
    n = qd.shape[0]
    rows = seq // GRID_W
    assert rows >= NAT_KR
    bias = _nat_bias_table(rpb)
    rps = NAT_ROWS_PER_STEP
    assert rows % rps == 0
    steps = rows // rps
    once = pl.Buffered(1)
    return pl.pallas_call(
        functools.partial(_nat_kernel, rows=rows, rows_per_step=rps),
        grid=(batch, steps),
        in_specs=[pl.BlockSpec((rps * GRID_W, 512), lambda b, r: (b * steps + r, 0)),
                  pl.BlockSpec((seq, 512), lambda b, r: (b, 0), pipeline_mode=once),
                  pl.BlockSpec((seq, 512), lambda b, r: (b, 0), pipeline_mode=once),
                  pl.BlockSpec(bias.shape, lambda b, r: (0, 0, 0, 0), pipeline_mode=once)],
        out_specs=pl.BlockSpec((rps * GRID_W, 512), lambda b, r: (b * steps + r, 0)),
        out_shape=jax.ShapeDtypeStruct((n, 512), BF16),
        compiler_params=_params("parallel", "arbitrary"),
        name="nat",
    )(qd, kd, vd, bias)


def _merge_kernel(x_ref, oa_ref, ob_ref, oc_ref, od_ref, sh_ref, sc_ref, gm_ref, wg_ref, wb_ref, wo_ref,
                  g_ref, b_ref, o_ref, *, alpha):
    x = x_ref[...]
    ub = (_norm_rows(x) * (1.0 + sc_ref[...]) + sh_ref[...]).astype(BF16)
    merged = None
    for n, br_ref in enumerate((oa_ref, ob_ref, oc_ref, od_ref)):
        term = _sigmoid(_bdot(ub, wg_ref[n])) * _bdot(br_ref[...], wb_ref[n])
        merged = term if merged is None else merged + term
    y = _bdot(merged.astype(BF16), wo_ref[...])
    z = alpha * x + gm_ref[...] * y
    o_ref[...] = _norm_rows(z) * g_ref[...] + b_ref[...]


def _merge(x2, branches, shift, scale, gate, wg, wb, wo, ln_g, ln_b, alpha, seq, tm):
    n, d = x2.shape
    spt = seq // tm
    row = lambda i: (i, 0)
    mod = lambda i: (i // spt, 0, 0)
    once = pl.Buffered(1)
    return pl.pallas_call(
        functools.partial(_merge_kernel, alpha=alpha),
        grid=(n // tm,),
        in_specs=[pl.BlockSpec((tm, d), row)] + [pl.BlockSpec((tm, 512), row)] * 4
                 + [pl.BlockSpec((None, 1, d), mod)] * 3
                 + [pl.BlockSpec(wg.shape, lambda i: (0, 0, 0), pipeline_mode=once),
                    pl.BlockSpec(wb.shape, lambda i: (0, 0, 0), pipeline_mode=once),
                    pl.BlockSpec(wo.shape, lambda i: (0, 0), pipeline_mode=once),
                    pl.BlockSpec((1, d), lambda i: (0, 0)), pl.BlockSpec((1, d), lambda i: (0, 0))],
        out_specs=pl.BlockSpec((tm, d), row),
        out_shape=jax.ShapeDtypeStruct((n, d), F32),
        compiler_params=_params("parallel"),
        name="merge",
    )(x2, *branches, shift, scale, gate, wg, wb, wo, ln_g, ln_b)


def _peer_candidate_blocks(k):
    blocks = []
    for i in range(k):
        need = min(k, (k + 1) // (i + 1))
        if need == 1:
            assert (k - i) % 8 == 0
            blocks.append((i, 1))
            break
        blocks.append((i, -(-need // 8) * 8))
    return tuple(blocks)


_PEER_CAND_BLOCKS = _peer_candidate_blocks(PEER_TOPK)
_PEER_CAND_ROWS = sum((PEER_TOPK - i) if w == 1 else w for i, w in _PEER_CAND_BLOCKS)


def _take_top(ws, count, on_max, with_rank):
    ws = list(ws)
    ranks = [jnp.full(w.shape, float(count), F32) if flag else None for w, flag in zip(ws, with_rank)]
    for r in range(count):
        for n, w in enumerate(ws):
            m = jnp.max(w, axis=0, keepdims=True)
            on_max(n, r, m)
            if r + 1 < count or ranks[n] is not None:
                hit = w == m
                if ranks[n] is not None:
                    ranks[n] = jnp.where(hit, float(r), ranks[n])
                if r + 1 < count:
                    ws[n] = jnp.where(hit, -jnp.inf, w)
    return ranks


def _peer_head_scores(h, keys_ref, q_scr):
    halves = []
    for p in range(2):
        idx = 2 * h + p
        rows = pl.multiple_of(idx * PEER_NKEYS, PEER_NKEYS)
        halves.append(_bdot(keys_ref[idx], q_scr[pl.ds(rows, PEER_NKEYS), :]))
    return halves


def _peer_select_head(h, halves, top_scr, cand_scr, mix_ref, rank_ref, p2_ref):
    k = PEER_TOPK

    def keep(p, r, m):
        top_scr[p, r:r + 1, :] = m

    _, rank = _take_top(halves, k + 1, keep, with_rank=(False, True))
    s1, s2 = halves
    a = top_scr[0, :k, :]
    b = top_scr[1, :k, :]
    row = 0
    for i, width in _PEER_CAND_BLOCKS:
        if width == 1:
            cand_scr[row:row + k - i, :] = a[i:k, :] + b[0:1, :]
            row += k - i
        else:
            cand_scr[row:row + width, :] = a[i:i + 1, :] + b[0:width, :]
            row += width
    best = a[0:1, :] + b[0:1, :]
    stats = {"z": jnp.zeros_like(best)}

    def tally(_, r, m):
        if r < k:
            stats["z"] = stats["z"] + jnp.exp(m - best)
        if r == k - 1:
            stats["last"] = m
        if r == k:
            stats["next"] = m

    _take_top([cand_scr[...]], k + 1, tally, with_rank=(False,))
    runner_up = jnp.maximum(stats["next"], jnp.maximum(top_scr[0, k:k + 1, :] + b[0:1, :],
                                                       a[0:1, :] + top_scr[1, k:k + 1, :]))
    bound = 0.5 * (stats["last"] + runner_up) - s1
    count = jnp.zeros_like(s1)
    for qq in range(k):
        count = jnp.where(b[qq:qq + 1, :] >= bound, float(qq + 1), count)
    mix_ref[h] = count + 0.5 * jnp.exp(s1 - a[0:1, :])
    rank_ref[h] = rank.astype(BF16)
    p2_ref[h] = (jnp.exp(s2 - b[0:1, :]) / stats["z"]).astype(BF16)


def _peer_kernel(x_ref, xn_ref, sh_ref, sc_ref, gf_ref, g_ref, b_ref, wqt_ref, keys_ref, u_ref, vt_ref, o_ref,
                 ub_scr, q_scr, mix_scr, rank_scr, p2_scr, top_scr, cand_scr, w_scr, acc_scr, *,
                 alpha, ipc, sub):
    g = pl.program_id(0)
    c = pl.program_id(1)
    nxt = g % 2
    cur = 1 - nxt
    nk = PEER_NKEYS
    k = PEER_TOPK

    @pl.when(jnp.logical_and(g == 0, c == 0))
    def _zero_first_slot():
        ub_scr[1] = jnp.zeros(ub_scr.shape[1:], ub_scr.dtype)
        mix_scr[1] = jnp.zeros(mix_scr.shape[1:], mix_scr.dtype)
        rank_scr[1] = jnp.zeros(rank_scr.shape[1:], rank_scr.dtype)
        p2_scr[1] = jnp.zeros(p2_scr.shape[1:], p2_scr.dtype)

    @pl.when(c == 0)
    def _next_tile_queries():
        u = _norm_rows(xn_ref[...]) * (1.0 + sc_ref[...]) + sh_ref[...]
        ub_scr[nxt] = jnp.transpose(u).astype(BF16)
        q_scr[...] = _bdot(wqt_ref[...], ub_scr[nxt]).astype(BF16)
        acc_scr[...] = jnp.zeros_like(acc_scr)

    ips = sub // nk
    n_sub = ipc // ips

    def hidden(sb):
        return _bdot(u_ref[sb * sub:(sb + 1) * sub, :], ub_scr[cur])

    halves = _peer_head_scores(c, keys_ref, q_scr)
    hids = [hidden(sb) for sb in range(n_sub)]
    _peer_select_head(c, halves, top_scr, cand_scr, mix_scr.at[nxt], rank_scr.at[nxt], p2_scr.at[nxt])

    partial_out = None
    for sb in range(n_sub):
        rows = slice(sb * sub, (sb + 1) * sub)
        hid = hids[sb]
        for il in range(ips):
            i = c * ipc + sb * ips + il
            gate = None
            for h in range(PEER_HEADS):
                mix_row = mix_scr[cur, h, pl.ds(i, 1), :]
                cnt_row = jnp.floor(mix_row)
                p1_row = ((mix_row - cnt_row) * 2.0).astype(BF16)
                chosen = rank_scr[cur, h] < cnt_row.astype(BF16)
                term = jnp.where(chosen, p2_scr[cur, h], 0.0) * p1_row
                gate = term if gate is None else gate + term
            hb = hid[il * nk:(il + 1) * nk, :]
            act = 0.5 * hb * (1.0 + lax.erf(hb * (2.0 ** -0.5)))
            w_scr[sb * sub + il * nk:sb * sub + (il + 1) * nk, :] = gate * act.astype(BF16)
        out = _bdot(vt_ref[:, rows], w_scr[rows, :])
        partial_out = out if partial_out is None else partial_out + out
    acc_scr[...] += partial_out

    @pl.when(jnp.logical_and(c == pl.num_programs(1) - 1, g > 0))
    def _finish():
        y = jnp.transpose(acc_scr[...])
        z = alpha * x_ref[...] + gf_ref[...] * y
        o_ref[...] = _norm_rows(z) * g_ref[...] + b_ref[...]


def _peer(x2, shift, scale, gate, ln_g, ln_b, wqt, keys, u_tab, vt_tab, alpha, seq, tt):
    n, d = x2.shape
    chunk, sub = PEER_CHUNK, PEER_SUB
    assert u_tab.shape[0] % chunk == 0 and chunk % sub == 0 and sub % PEER_NKEYS == 0
    spt = seq // tt
    experts = u_tab.shape[0]
    ipc = chunk // PEER_NKEYS
    nhp = 2 * PEER_HEADS
    n_tiles = n // tt
    assert experts // chunk == PEER_HEADS
    prev = lambda g: jnp.maximum(g - 1, 0)
    this = lambda g: jnp.minimum(g, n_tiles - 1)
    once = pl.Buffered(1)
    return pl.pallas_call(
        functools.partial(_peer_kernel, alpha=alpha, ipc=ipc, sub=sub),
        grid=(n_tiles + 1, experts // chunk),
        in_specs=[pl.BlockSpec((tt, d), lambda g, c: (prev(g), 0), pipeline_mode=once),
                  pl.BlockSpec((tt, d), lambda g, c: (this(g), 0), pipeline_mode=once),
                  pl.BlockSpec((None, 1, d), lambda g, c: (this(g) // spt, 0, 0)),
                  pl.BlockSpec((None, 1, d), lambda g, c: (this(g) // spt, 0, 0)),
                  pl.BlockSpec((None, 1, d), lambda g, c: (prev(g) // spt, 0, 0)),
                  pl.BlockSpec((1, d), lambda g, c: (0, 0)), pl.BlockSpec((1, d), lambda g, c: (0, 0)),
                  pl.BlockSpec(wqt.shape, lambda g, c: (0, 0), pipeline_mode=once),
                  pl.BlockSpec(keys.shape, lambda g, c: (0, 0, 0), pipeline_mode=once),
                  pl.BlockSpec((chunk, d), lambda g, c: (c, 0)),
                  pl.BlockSpec((d, chunk), lambda g, c: (0, c))],
        out_specs=pl.BlockSpec((tt, d), lambda g, c: (prev(g), 0)),
        out_shape=jax.ShapeDtypeStruct((n, d), F32),
        scratch_shapes=[pltpu.VMEM((2, d, tt), BF16),
                        pltpu.VMEM((nhp * PEER_NKEYS, tt), BF16),
                        pltpu.VMEM((2, PEER_HEADS, PEER_NKEYS, tt), F32),
                        pltpu.VMEM((2, PEER_HEADS, PEER_NKEYS, tt), BF16),
                        pltpu.VMEM((2, PEER_HEADS, PEER_NKEYS, tt), BF16),
                        pltpu.VMEM((2, PEER_TOPK + 8, tt), F32),
                        pltpu.VMEM((_PEER_CAND_ROWS, tt), F32),
                        pltpu.VMEM((chunk, tt), BF16),
                        pltpu.VMEM((d, tt), F32)],
        compiler_params=_params("arbitrary", "arbitrary"),
        name="peer",
    )(x2, x2, shift, scale, gate, ln_g, ln_b, wqt, keys, u_tab, vt_tab)


def _rotate_half_cols(w, dim):
    rows, cols = w.shape
    w3 = w.reshape(rows, cols // dim, dim)
    return jnp.concatenate([-w3[..., dim // 2:], w3[..., :dim // 2]], axis=-1).reshape(rows, cols)


def _pad_cols(w, width):
    return jnp.pad(w, ((0, 0), (0, width - w.shape[1])))


def _in_proj_weights(w_in):
    widths = (512, 128, 128, 384, 256, 32, 512, 512, 512, 512, 512, 512)
    names = ("qa", "ka", "va", "cq", "ckv", "kr", "qc", "kc", "vc", "qd", "kd", "vd")
    parts, off = {}, 0
    for name, width in zip(names, widths):
        parts[name] = w_in[:, off:off + width]
        off += width
    for name, extra in (("qa", 1.0), ("qc", LOG2_E), ("qd", 1.0)):
        parts[name] = parts[name] * (HEAD_DIM ** -0.5 * extra)
    rot_dim = {"qa": HEAD_DIM, "ka": HEAD_DIM, "qc": DIFF_QK, "kc": DIFF_QK, "kr": MLA_ROPE}
    cols = [_pad_cols(parts[name], width) for name, width in _MAIN_GROUPS]
    cols += [_pad_cols(_rotate_half_cols(parts[name], rot_dim[name]), width) for name, width in _ROT_GROUPS]
    return jnp.concatenate(cols, axis=1).astype(BF16)


def _mla_weights(q_norm, q_up, kv_norm, kv_up):
    qr, kvr = q_up.shape[0], kv_up.shape[0]
    qh = q_up.reshape(qr, MLA_HEADS, MLA_NOPE + MLA_ROPE) * ((MLA_NOPE + MLA_ROPE) ** -0.5 * LOG2_E)
    pad = MLA_SLOT - MLA_NOPE - MLA_ROPE
    wq = jnp.pad(qh, ((0, 0), (0, 0), (0, pad))).reshape(qr, MLA_HEADS * MLA_SLOT)
    rot = _rotate_half_cols(qh[..., MLA_NOPE:].reshape(qr, MLA_HEADS * MLA_ROPE), MLA_ROPE)
    rot = rot.reshape(qr, MLA_HEADS, MLA_ROPE)
    wqr = jnp.pad(rot, ((0, 0), (0, 0), (MLA_NOPE, pad))).reshape(qr, MLA_HEADS * MLA_SLOT)
    kvh = kv_up.reshape(kvr, MLA_HEADS, MLA_NOPE + MLA_V)
    wk = jnp.pad(kvh[..., :MLA_NOPE], ((0, 0), (0, 0), (0, MLA_SLOT - MLA_NOPE))).reshape(kvr, MLA_HEADS * MLA_SLOT)
    wv = kvh[..., MLA_NOPE:].reshape(kvr, MLA_HEADS * MLA_V)
    place = np.zeros((128, MLA_HEADS, MLA_SLOT), np.float32)
    for r in range(MLA_ROPE):
        place[r, :, MLA_NOPE + r] = 1.0
    return {"qn": q_norm.reshape(1, qr), "kvn": kv_norm.reshape(1, kvr),
            "wq": wq.astype(BF16), "wqr": wqr.astype(BF16), "wk": wk.astype(BF16), "wv": wv.astype(BF16),
            "e": jnp.asarray(place.reshape(128, MLA_HEADS * MLA_SLOT), BF16)}


def _rope_tables(seq):
    def base(dim):
        inv = ROPE_THETA ** (-jnp.arange(0, dim, 2, dtype=F32) / dim)
        ang = jnp.arange(seq, dtype=F32)[:, None] * inv[None, :]
        cos, sin = jnp.cos(ang), jnp.sin(ang)
        return jnp.concatenate([cos, cos], axis=1), jnp.concatenate([sin, sin], axis=1)

    cos64, sin64 = base(HEAD_DIM)
    cos32, sin32 = base(MLA_ROPE)
    ones = jnp.ones((seq, MLA_NOPE), F32)
    zeros = jnp.zeros((seq, MLA_NOPE), F32)
    tail = jnp.zeros((seq, MLA_SLOT - MLA_NOPE - MLA_ROPE), F32)
    return {"cos64": jnp.concatenate([cos64, cos64], axis=1), "sin64": jnp.concatenate([sin64, sin64], axis=1),
            "cos32": _pad_cols(cos32, 128), "sin32": _pad_cols(sin32, 128),
            "mla_cos": jnp.concatenate([ones, cos32, tail], axis=1),
            "mla_sin": jnp.concatenate([zeros, sin32, tail], axis=1)}


def _tile(total, want):
    t = min(total, want)
    assert total % t == 0
    return t


def kernel(x, c, ada_w, ada_b, w_in, swa_sink, mla_q_norm, mla_q_up, mla_kv_norm, mla_kv_up, diff_lambda_q1, diff_lambda_k1, diff_lambda_q2, diff_lambda_k2, diff_subln, nat_rpb, w_gate, w_branch, w_out, ln1_g, ln1_b, peer_wq, peer_keys, peer_u, peer_v, ln2_g, ln2_b):
    batch, seq, d = x.shape
    depth = ada_w.shape[0]
    alpha = (2 * depth) ** 0.25
    n = batch * seq
    tabs = _rope_tables(seq)
    mod = _ada(c, ada_w, ada_b)
    tm = _tile(seq, 512)
    x2 = x.reshape(n, d)
    for l in range(depth):
        sh_mix, sc_mix, g_mix, sh_ffn, sc_ffn, g_ffn = (mod[l, :, j] for j in range(6))
        (qa, ka, va, cq, ckv, kr, qc, kc, vc, qd, kd, vd) = _in_proj(
            x2, sh_mix, sc_mix, _in_proj_weights(w_in[l]), tabs, seq, tm)
        o_a = _swa(qa, ka, va, swa_sink[l], batch, seq, _tile(seq, SWA_TQ))
        qm, km, vm = _mla_prep(cq, ckv, kr, _mla_weights(mla_q_norm[l], mla_q_up[l], mla_kv_norm[l], mla_kv_up[l]),
                               tabs, seq, tm)
        o_b = _mla(qm, km, vm, batch, seq, _tile(seq, FLASH_TQ), _tile(seq, FLASH_TK))
        lambda_init = 0.8 - 0.6 * math.exp(-0.3 * l)
        lams = tuple(v[l].reshape(1, DIFF_QK) for v in (diff_lambda_q1, diff_lambda_k1, diff_lambda_q2, diff_lambda_k2))
        o_c = _diff(qc, kc, vc, lams, diff_subln[l].reshape(1, DIFF_V), lambda_init, batch, seq,
                    _tile(seq, FLASH_TQ), _tile(seq, FLASH_TK))
        o_d = _nat(qd, kd, vd, nat_rpb[l], batch, seq)
        x2 = _merge(x2, (o_a, o_b, o_c, o_d), sh_mix, sc_mix, g_mix, w_gate[l].astype(BF16),
                    w_branch[l].astype(BF16), w_out[l].astype(BF16), ln1_g[l].reshape(1, d), ln1_b[l].reshape(1, d),
                    alpha, seq, tm)
        keys = peer_keys[l].reshape(2 * PEER_HEADS, PEER_NKEYS, PEER_DKEY // 2).astype(BF16)
        x2 = _peer(x2, sh_ffn, sc_ffn, g_ffn, ln2_g[l].reshape(1, d), ln2_b[l].reshape(1, d),
                   peer_wq[l].T.astype(BF16), keys, peer_u[l].astype(BF16), peer_v[l].T.astype(BF16),
                   alpha, seq, tm)
    return x2.reshape(batch, seq, d)
```

```python
import functools
import math

import jax
import jax.numpy as jnp
import numpy as np
from jax import lax
from jax.experimental import pallas as pl
from jax.experimental.pallas import tpu as pltpu

F32 = jnp.float32
BF16 = jnp.bfloat16

GRID_W = 64
ROPE_THETA = 10000.0
HEAD_DIM = 64
LN_EPS = 1e-5
NEG_INF = -1e30
LOG2_E = 1.4426950408889634

SWA_HEADS = 8
SWA_KV_HEADS = 2
SWA_WINDOW = 128
MLA_HEADS = 8
MLA_NOPE = 64
MLA_ROPE = 32
MLA_V = 64
MLA_SLOT = 128
DIFF_HEADS = 4
DIFF_QK = 64
DIFF_V = 128
NAT_HEADS = 8
NAT_KR = 8
NAT_KC = 16
PEER_HEADS = 8
PEER_NKEYS = 128
PEER_DKEY = 256
PEER_TOPK = 16

VMEM_LIMIT_BYTES = 56 * 1024 * 1024
FLASH_TQ = 2048
FLASH_Q_BLOCKS = 4
FLASH_TK = 1024
SWA_TQ = 256
NAT_ROWS_PER_STEP = 4
PEER_CHUNK = 2048
PEER_SUB = 512

_MAIN_GROUPS = (("qa", 512), ("ka", 128), ("va", 128), ("cq", 384), ("ckv", 256), ("qc", 512), ("kc", 512),
                ("vc", 512), ("qd", 512), ("kd", 512), ("vd", 512), ("kr", 128))
_ROT_GROUPS = (("qa", 512), ("ka", 128), ("qc", 512), ("kc", 512), ("kr", 128))


def _offsets(groups, base=0):
    out, off = {}, base
    for name, width in groups:
        out[name] = (off, off + width)
        off += width
    return out, off


_MAIN_OFF, _MAIN_END = _offsets(_MAIN_GROUPS)
_ROT_OFF, _W_ALL_COLS = _offsets(_ROT_GROUPS, _MAIN_END)

_NT = (((1,), (1,)), ((), ()))


def _params(*sem):
    return pltpu.CompilerParams(dimension_semantics=sem, vmem_limit_bytes=VMEM_LIMIT_BYTES)


def _norm_rows(x):
    mu = jnp.mean(x, axis=-1, keepdims=True)
    xc = x - mu
    var = jnp.mean(xc * xc, axis=-1, keepdims=True)
    return xc * lax.rsqrt(var + LN_EPS)


def _sigmoid(z):
    return 1.0 / (1.0 + jnp.exp(-z))


def _bdot(a, b):
    return jnp.dot(a, b, preferred_element_type=F32)


def _ada_kernel(c_ref, w_ref, b_ref, o_ref):
    c = c_ref[...]
    act = c * _sigmoid(c)
    o_ref[...] = jnp.dot(act, w_ref[...], preferred_element_type=F32,
                         precision=lax.Precision.HIGHEST) + b_ref[...]


def _ada(c, ada_w, ada_b):
    depth, d, six_d = ada_w.shape
    b = c.shape[0]
    rows = -(-b // 8) * 8
    c_pad = jnp.pad(c, ((0, rows - b), (0, 0)))
    out = pl.pallas_call(
        _ada_kernel,
        grid=(depth, six_d // d),
        in_specs=[pl.BlockSpec((rows, d), lambda l, j: (0, 0)),
                  pl.BlockSpec((None, d, d), lambda l, j: (l, 0, j)),
                  pl.BlockSpec((None, 1, d), lambda l, j: (l, 0, j))],
        out_specs=pl.BlockSpec((None, rows, d), lambda l, j: (l, 0, j)),
        out_shape=jax.ShapeDtypeStruct((depth, rows, six_d), F32),
        compiler_params=_params("parallel", "parallel"),
        name="ada",
    )(c_pad, ada_w, ada_b.reshape(depth, 1, six_d))
    return out[:, :b].reshape(depth, b, 6, 1, d)


def _in_proj_kernel(x_ref, sh_ref, sc_ref, w_ref, cos_ref, sin_ref, cos32_ref, sin32_ref,
                    qa_ref, ka_ref, va_ref, cq_ref, ckv_ref, kr_ref, qc_ref, kc_ref, vc_ref,
                    qd_ref, kd_ref, vd_ref):
    u = _norm_rows(x_ref[...]) * (1.0 + sc_ref[...]) + sh_ref[...]
    ub = u.astype(BF16)

    def proj(off):
        return _bdot(ub, w_ref[:, off[0]:off[1]])

    def roped(name, cos, sin):
        main, rot = proj(_MAIN_OFF[name]), proj(_ROT_OFF[name])
        reps = main.shape[1] // cos.shape[1]
        if reps > 1:
            cos = jnp.concatenate([cos] * reps, axis=1)
            sin = jnp.concatenate([sin] * reps, axis=1)
        return main * cos + rot * sin

    cos, sin = cos_ref[...], sin_ref[...]
    qa_ref[...] = roped("qa", cos, sin).astype(qa_ref.dtype)
    ka_ref[...] = roped("ka", cos, sin).astype(ka_ref.dtype)
    va_ref[...] = proj(_MAIN_OFF["va"]).astype(va_ref.dtype)
    cq_ref[...] = proj(_MAIN_OFF["cq"])
    ckv_ref[...] = proj(_MAIN_OFF["ckv"])
    kr_ref[...] = roped("kr", cos32_ref[...], sin32_ref[...]).astype(kr_ref.dtype)
    qc_ref[...] = roped("qc", cos, sin).astype(qc_ref.dtype)
    kc_ref[...] = roped("kc", cos, sin).astype(kc_ref.dtype)
    vc_ref[...] = proj(_MAIN_OFF["vc"]).astype(vc_ref.dtype)
    qd_ref[...] = proj(_MAIN_OFF["qd"]).astype(qd_ref.dtype)
    kd_ref[...] = proj(_MAIN_OFF["kd"]).astype(kd_ref.dtype)
    vd_ref[...] = proj(_MAIN_OFF["vd"]).astype(vd_ref.dtype)


def _in_proj(x2, shift, scale, w_all, tabs, seq, tm):
    n, d = x2.shape
    spt = seq // tm
    widths = dict(_MAIN_GROUPS)
    names = ("qa", "ka", "va", "cq", "ckv", "kr", "qc", "kc", "vc", "qd", "kd", "vd")
    dtypes = {k: BF16 for k in names}
    dtypes["cq"] = F32
    dtypes["ckv"] = F32
    row = lambda i: (i, 0)
    mod = lambda i: (i // spt, 0, 0)
    tab = lambda i: (i % spt, 0)
    return pl.pallas_call(
        _in_proj_kernel,
        grid=(n // tm,),
        in_specs=[pl.BlockSpec((tm, d), row),
                  pl.BlockSpec((None, 1, d), mod),
                  pl.BlockSpec((None, 1, d), mod),
                  pl.BlockSpec((d, _W_ALL_COLS), lambda i: (0, 0), pipeline_mode=pl.Buffered(1)),
                  pl.BlockSpec((tm, 128), tab), pl.BlockSpec((tm, 128), tab),
                  pl.BlockSpec((tm, 128), tab), pl.BlockSpec((tm, 128), tab)],
        out_specs=[pl.BlockSpec((tm, widths[k]), row) for k in names],
        out_shape=[jax.ShapeDtypeStruct((n, widths[k]), dtypes[k]) for k in names],
        compiler_params=_params("parallel"),
        name="in_proj",
    )(x2, shift, scale, w_all, tabs["cos64"], tabs["sin64"], tabs["cos32"], tabs["sin32"])


def _swa_kernel(q_ref, k_ref, v_ref, sink_ref, o_ref, *, tq, seq):
    i = pl.program_id(1)
    span = tq + 2 * SWA_WINDOW
    start = pl.multiple_of(jnp.clip(i * tq - SWA_WINDOW, 0, seq - span), SWA_WINDOW)
    ks = k_ref[pl.ds(start, span), :]
    vs = v_ref[pl.ds(start, span), :]
    qpos = i * tq + lax.broadcasted_iota(jnp.int32, (tq, span), 0)
    kpos = start + lax.broadcasted_iota(jnp.int32, (tq, span), 1)
    valid = jnp.abs(qpos - kpos) <= SWA_WINDOW
    q = q_ref[...]
    grp = SWA_HEADS // SWA_KV_HEADS
    outs = []
    for h in range(SWA_HEADS):
        kv = h // grp
        qh = q[:, h * HEAD_DIM:(h + 1) * HEAD_DIM]
        kh = ks[:, kv * HEAD_DIM:(kv + 1) * HEAD_DIM]
        vh = vs[:, kv * HEAD_DIM:(kv + 1) * HEAD_DIM]
        s = lax.dot_general(qh, kh, _NT, preferred_element_type=F32)
        s = jnp.where(valid, s, NEG_INF)
        sink = sink_ref[h][:, :1]
        m = jnp.maximum(jnp.max(s, axis=-1, keepdims=True), sink)
        p = jnp.exp(s - m)
        denom = jnp.sum(p, axis=-1, keepdims=True) + jnp.exp(sink - m)
        outs.append(_bdot(p.astype(BF16), vh) / denom)
    o_ref[...] = jnp.concatenate(outs, axis=1).astype(o_ref.dtype)


def _swa(qa, ka, va, sink, batch, seq, tq):
    n = qa.shape[0]
    nq = seq // tq
    sink_b = jnp.broadcast_to(sink.astype(F32)[:, None, None], (SWA_HEADS, 1, 128))
    return pl.pallas_call(
        functools.partial(_swa_kernel, tq=tq, seq=seq),
        grid=(batch, nq),
        in_specs=[pl.BlockSpec((tq, 512), lambda b, i: (b * nq + i, 0)),
                  pl.BlockSpec((seq, ka.shape[1]), lambda b, i: (b, 0)),
                  pl.BlockSpec((seq, va.shape[1]), lambda b, i: (b, 0)),
                  pl.BlockSpec((SWA_HEADS, 1, 128), lambda b, i: (0, 0, 0))],
        out_specs=pl.BlockSpec((tq, 512), lambda b, i: (b * nq + i, 0)),
        out_shape=jax.ShapeDtypeStruct((n, 512), BF16),
        compiler_params=_params("parallel", "parallel"),
        name="swa",
    )(qa, ka, va, sink_b)


def _mla_prep_kernel(cq_ref, ckv_ref, kr_ref, qn_ref, kvn_ref, wq_ref, wqr_ref, wk_ref, e_ref, wv_ref,
                     ct_ref, st_ref, qm_ref, km_ref, vm_ref):
    def rms(x, g):
        return (x * lax.rsqrt(jnp.mean(x * x, axis=-1, keepdims=True) + LN_EPS) * g).astype(BF16)

    qn = rms(cq_ref[...], qn_ref[...])
    kvn = rms(ckv_ref[...], kvn_ref[...])
    cos = jnp.concatenate([ct_ref[...]] * MLA_HEADS, axis=1)
    sin = jnp.concatenate([st_ref[...]] * MLA_HEADS, axis=1)
    qm_ref[...] = (_bdot(qn, wq_ref[...]) * cos + _bdot(qn, wqr_ref[...]) * sin).astype(qm_ref.dtype)
    km_ref[...] = (_bdot(kvn, wk_ref[...]) + _bdot(kr_ref[...], e_ref[...])).astype(km_ref.dtype)
    vm_ref[...] = _bdot(kvn, wv_ref[...]).astype(vm_ref.dtype)


def _mla_prep(cq, ckv, kr, w, tabs, seq, tm):
    n = cq.shape[0]
    spt = seq // tm
    row = lambda i: (i, 0)
    full = lambda i: (0, 0)
    tab = lambda i: (i % spt, 0)
    hw = MLA_HEADS * MLA_SLOT
    return pl.pallas_call(
        _mla_prep_kernel,
        grid=(n // tm,),
        in_specs=[pl.BlockSpec((tm, cq.shape[1]), row), pl.BlockSpec((tm, ckv.shape[1]), row),
                  pl.BlockSpec((tm, 128), row),
                  pl.BlockSpec((1, cq.shape[1]), full), pl.BlockSpec((1, ckv.shape[1]), full),
                  pl.BlockSpec(w["wq"].shape, full), pl.BlockSpec(w["wqr"].shape, full),
                  pl.BlockSpec(w["wk"].shape, full), pl.BlockSpec(w["e"].shape, full),
                  pl.BlockSpec(w["wv"].shape, full),
                  pl.BlockSpec((tm, 128), tab), pl.BlockSpec((tm, 128), tab)],
        out_specs=[pl.BlockSpec((tm, hw), row), pl.BlockSpec((tm, hw), row),
                   pl.BlockSpec((tm, MLA_HEADS * MLA_V), row)],
        out_shape=[jax.ShapeDtypeStruct((n, hw), BF16), jax.ShapeDtypeStruct((n, hw), BF16),
                   jax.ShapeDtypeStruct((n, MLA_HEADS * MLA_V), BF16)],
        compiler_params=_params("parallel"),
        name="mla_prep",
    )(cq, ckv, kr, w["qn"], w["kvn"], w["wq"], w["wqr"], w["wk"], w["e"], w["wv"],
      tabs["mla_cos"], tabs["mla_sin"])


def _flash_chains(qs, k_ref, vt_ref, k_lanes, seq, tk, with_ones, kinds, ones_row):
    n_rows = [v.shape[0] for v in jax.eval_shape(with_ones, jax.ShapeDtypeStruct(vt_ref.shape[1:], vt_ref.dtype))]

    def step(kt, carry):
        off = pl.multiple_of(kt * tk, tk)
        variants = with_ones(vt_ref[kt])
        out = []
        scores = [lax.dot_general(k_ref[pl.ds(off, tk), lanes[0]:lanes[1]], q, _NT, preferred_element_type=F32)
                  for q, lanes in zip(qs, k_lanes)]
        for s, kind, (m, acc) in zip(scores, kinds, carry):
            m_new = jnp.maximum(m, jnp.max(s, axis=0, keepdims=True))
            alpha = jnp.exp2(m - m_new)
            p = jnp.exp2(s - m_new).astype(BF16)
            acc = alpha * acc + _bdot(variants[kind], p)
            out.append((m_new, acc))
        return tuple(out)

    init = tuple((jnp.full((1, q.shape[0]), NEG_INF, F32), jnp.zeros((n_rows[kind], q.shape[0]), F32))
                 for q, kind in zip(qs, kinds))
    final = lax.fori_loop(0, seq // tk, step, init)
    return [(acc, acc[row:row + 1, :]) for (_, acc), row in zip(final, ones_row)]


def _values_transposed(v, batch, seq, tk):
    groups = v.shape[1] // 128
    v5 = v.reshape(batch, seq // tk, tk, groups, 128)
    return jnp.transpose(v5, (0, 3, 1, 4, 2))


def _mla_kernel(q_ref, k_ref, vt_ref, o_ref, *, seq, tk):
    rows = q_ref.shape[0] // FLASH_Q_BLOCKS
    lanes = [(hh * MLA_SLOT, (hh + 1) * MLA_SLOT) for hh in range(2)]
    qs = [q_ref[r * rows:(r + 1) * rows, lo:hi] for r in range(FLASH_Q_BLOCKS) for lo, hi in lanes]

    def with_ones(vt):
        vrow = lax.broadcasted_iota(jnp.int32, vt.shape, 0)
        one = jnp.ones_like(vt)
        return [jnp.where(vrow < MLA_V, vt, one), jnp.where(vrow < MLA_V, one, vt)]

    res = _flash_chains(qs, k_ref, vt_ref, lanes * FLASH_Q_BLOCKS, seq, tk, with_ones,
                        kinds=[0, 1] * FLASH_Q_BLOCKS, ones_row=[MLA_V, 0] * FLASH_Q_BLOCKS)
    outs = [acc / l for acc, l in res]
    row = lax.broadcasted_iota(jnp.int32, outs[0].shape, 0)
    for r in range(FLASH_Q_BLOCKS):
        o_t = jnp.where(row < MLA_V, outs[2 * r], outs[2 * r + 1])
        o_ref[r * rows:(r + 1) * rows, :] = jnp.transpose(o_t).astype(o_ref.dtype)


def _mla(qm, km, vm, batch, seq, tq, tk):
    n = qm.shape[0]
    nq = seq // tq
    pairs = MLA_HEADS // 2
    return pl.pallas_call(
        functools.partial(_mla_kernel, seq=seq, tk=tk),
        grid=(batch, pairs, nq),
        in_specs=[pl.BlockSpec((tq, 2 * MLA_SLOT), lambda b, h, i: (b * nq + i, h)),
                  pl.BlockSpec((seq, 2 * MLA_SLOT), lambda b, h, i: (b, h)),
                  pl.BlockSpec((None, None, seq // tk, 2 * MLA_V, tk), lambda b, h, i: (b, h, 0, 0, 0))],
        out_specs=pl.BlockSpec((tq, 2 * MLA_V), lambda b, h, i: (b * nq + i, h)),
        out_shape=jax.ShapeDtypeStruct((n, MLA_HEADS * MLA_V), BF16),
        compiler_params=_params("parallel", "parallel", "parallel"),
        name="mla",
    )(qm, km, _values_transposed(vm, batch, seq, tk))


def _diff_kernel(q_ref, k_ref, vt_ref, lq1_ref, lk1_ref, lq2_ref, lk2_ref, sub_ref, o_ref, *,
                 seq, tk, lambda_init):
    rows = q_ref.shape[0] // FLASH_Q_BLOCKS
    lane = lax.broadcasted_iota(jnp.int32, (rows, 2 * DIFF_QK), 1)
    qs = []
    for r in range(FLASH_Q_BLOCKS):
        q = q_ref[r * rows:(r + 1) * rows, :]
        zero = jnp.zeros_like(q)
        qs += [jnp.where(lane < DIFF_QK, q, zero), jnp.where(lane >= DIFF_QK, q, zero)]

    def with_ones(vt):
        return [jnp.concatenate([vt, jnp.ones((16, vt.shape[1]), vt.dtype)], axis=0)]

    res = _flash_chains(qs, k_ref, vt_ref, [(0, 2 * DIFF_QK)] * len(qs), seq, tk, with_ones,
                        kinds=[0] * len(qs), ones_row=[DIFF_V] * len(qs))
    lam = (jnp.exp(jnp.sum(lq1_ref[...] * lk1_ref[...], axis=-1, keepdims=True))
           - jnp.exp(jnp.sum(lq2_ref[...] * lk2_ref[...], axis=-1, keepdims=True)) + lambda_init)
    for r in range(FLASH_Q_BLOCKS):
        (acc1, l1), (acc2, l2) = res[2 * r], res[2 * r + 1]
        o = jnp.transpose(acc1[:DIFF_V] / l1 - lam * (acc2[:DIFF_V] / l2))
        o = o * lax.rsqrt(jnp.mean(o * o, axis=-1, keepdims=True) + LN_EPS) * sub_ref[...]
        o_ref[r * rows:(r + 1) * rows, :] = (o * (1.0 - lambda_init)).astype(o_ref.dtype)


def _diff(qc, kc, vc, lams, subln, lambda_init, batch, seq, tq, tk):
    n = qc.shape[0]
    nq = seq // tq
    vec = lambda b, h, i: (0, 0)
    return pl.pallas_call(
        functools.partial(_diff_kernel, seq=seq, tk=tk, lambda_init=lambda_init),
        grid=(batch, DIFF_HEADS, nq),
        in_specs=[pl.BlockSpec((tq, 2 * DIFF_QK), lambda b, h, i: (b * nq + i, h)),
                  pl.BlockSpec((seq, 2 * DIFF_QK), lambda b, h, i: (b, h)),
                  pl.BlockSpec((None, None, seq // tk, DIFF_V, tk), lambda b, h, i: (b, h, 0, 0, 0)),
                  pl.BlockSpec((1, DIFF_QK), vec), pl.BlockSpec((1, DIFF_QK), vec),
                  pl.BlockSpec((1, DIFF_QK), vec), pl.BlockSpec((1, DIFF_QK), vec),
                  pl.BlockSpec((1, DIFF_V), vec)],
        out_specs=pl.BlockSpec((tq, DIFF_V), lambda b, h, i: (b * nq + i, h)),
        out_shape=jax.ShapeDtypeStruct((n, DIFF_HEADS * DIFF_V), BF16),
        compiler_params=_params("parallel", "parallel", "parallel"),
        name="diff",
    )(qc, kc, _values_transposed(vc, batch, seq, tk), *lams, subln)


def _nat_row_start(r, rows):
    return jnp.clip(r - NAT_KR // 2, 0, rows - NAT_KR)


def _nat_kernel(q_ref, k_ref, v_ref, bias_ref, o_ref, *, rows, rows_per_step):
    pairs = [slice(pp * 2 * HEAD_DIM, (pp + 1) * 2 * HEAD_DIM) for pp in range(NAT_HEADS // 2)]
    lane = lax.broadcasted_iota(jnp.int32, (GRID_W, 2 * HEAD_DIM), 1)
    values, logits = [], []
    for rr in range(rows_per_step):
        r = pl.program_id(1) * rows_per_step + rr
        r0 = _nat_row_start(r, rows)
        band = r0 - r + (NAT_KR - 1)
        off = pl.multiple_of(r0 * GRID_W, GRID_W)
        ks = k_ref[pl.ds(off, NAT_KR * GRID_W), :]
        values.append(v_ref[pl.ds(off, NAT_KR * GRID_W), :])
        q = q_ref[rr * GRID_W:(rr + 1) * GRID_W, :]
        row_logits = []
        for pp, sl in enumerate(pairs):
            qp = q[:, sl]
            zero = jnp.zeros_like(qp)
            for half, qh in enumerate((jnp.where(lane < HEAD_DIM, qp, zero), jnp.where(lane < HEAD_DIM, zero, qp))):
                row_logits.append(lax.dot_general(qh, ks[:, sl], _NT, preferred_element_type=F32)
                                  + bias_ref[band, 2 * pp + half])
        logits.append(row_logits)
    probs = []
    for row_logits in logits:
        row_probs = []
        for s in row_logits:
            p = jnp.exp(s - jnp.max(s, axis=-1, keepdims=True))
            row_probs.append((p.astype(BF16), jnp.sum(p, axis=-1, keepdims=True)))
        probs.append(row_probs)
    for rr, (row_probs, vs) in enumerate(zip(probs, values)):
        outs = []
        for pp, sl in enumerate(pairs):
            (p0, d0), (p1, d1) = row_probs[2 * pp], row_probs[2 * pp + 1]
            outs.append(jnp.where(lane < HEAD_DIM, _bdot(p0, vs[:, sl]) / d0, _bdot(p1, vs[:, sl]) / d1))
        o_ref[rr * GRID_W:(rr + 1) * GRID_W, :] = jnp.concatenate(outs, axis=1).astype(o_ref.dtype)


def _nat_bias_table(rpb):
    heads, n_row_off, n_col_off = rpb.shape
    cols = np.arange(GRID_W)
    col_start = np.clip(cols - NAT_KC // 2, 0, GRID_W - NAT_KC)
    inside = (cols[None, :] >= col_start[:, None]) & (cols[None, :] < col_start[:, None] + NAT_KC)
    period = 2 * GRID_W
    lead = GRID_W - NAT_KC
    sig = jnp.pad(rpb.astype(F32), ((0, 0), (0, 0), (lead, period - n_col_off - lead)))
    flat = jnp.tile(sig, (1, 1, GRID_W))[..., :GRID_W * (period - 1)]
    toep = flat.reshape(heads, n_row_off, GRID_W, period - 1)[..., GRID_W - 1:]
    toep = jnp.where(inside[None, None], toep, NEG_INF)
    bands = [jnp.transpose(toep[:, d:d + NAT_KR], (0, 2, 1, 3)).reshape(heads, GRID_W, NAT_KR * GRID_W)
             for d in range(NAT_KR)]
    return jnp.stack(bands, axis=0)


def _nat(qd, kd, vd, rpb, batch, seq):
    n = qd.shape[0]
    rows = seq // GRID_W
    assert rows >= NAT_KR
    bias = _nat_bias_table(rpb)
    rps = NAT_ROWS_PER_STEP
    assert rows % rps == 0
    steps = rows // rps
    once = pl.Buffered(1)
    return pl.pallas_call(
        functools.partial(_nat_kernel, rows=rows, rows_per_step=rps),
        grid=(batch, steps),
        in_specs=[pl.BlockSpec((rps * GRID_W, 512), lambda b, r: (b * steps + r, 0)),
                  pl.BlockSpec((seq, 512), lambda b, r: (b, 0), pipeline_mode=once),
                  pl.BlockSpec((seq, 512), lambda b, r: (b, 0), pipeline_mode=once),
                  pl.BlockSpec(bias.shape, lambda b, r: (0, 0, 0, 0), pipeline_mode=once)],
        out_specs=pl.BlockSpec((rps * GRID_W, 512), lambda b, r: (b * steps + r, 0)),
        out_shape=jax.ShapeDtypeStruct((n, 512), BF16),
        compiler_params=_params("parallel", "arbitrary"),
        name="nat",
    )(qd, kd, vd, bias)


def _merge_kernel(x_ref, oa_ref, ob_ref, oc_ref, od_ref, sh_ref, sc_ref, gm_ref, wg_ref, wb_ref, wo_ref,
                  g_ref, b_ref, o_ref, *, alpha):
    x = x_ref[...]
    ub = (_norm_rows(x) * (1.0 + sc_ref[...]) + sh_ref[...]).astype(BF16)
    merged = None
    for n, br_ref in enumerate((oa_ref, ob_ref, oc_ref, od_ref)):
        term = _sigmoid(_bdot(ub, wg_ref[n])) * _bdot(br_ref[...], wb_ref[n])
        merged = term if merged is None else merged + term
    y = _bdot(merged.astype(BF16), wo_ref[...])
    z = alpha * x + gm_ref[...] * y
    o_ref[...] = _norm_rows(z) * g_ref[...] + b_ref[...]


def _merge(x2, branches, shift, scale, gate, wg, wb, wo, ln_g, ln_b, alpha, seq, tm):
    n, d = x2.shape
    spt = seq // tm
    row = lambda i: (i, 0)
    mod = lambda i: (i // spt, 0, 0)
    once = pl.Buffered(1)
    return pl.pallas_call(
        functools.partial(_merge_kernel, alpha=alpha),
        grid=(n // tm,),
        in_specs=[pl.BlockSpec((tm, d), row)] + [pl.BlockSpec((tm, 512), row)] * 4
                 + [pl.BlockSpec((None, 1, d), mod)] * 3
                 + [pl.BlockSpec(wg.shape, lambda i: (0, 0, 0), pipeline_mode=once),
                    pl.BlockSpec(wb.shape, lambda i: (0, 0, 0), pipeline_mode=once),
                    pl.BlockSpec(wo.shape, lambda i: (0, 0), pipeline_mode=once),
                    pl.BlockSpec((1, d), lambda i: (0, 0)), pl.BlockSpec((1, d), lambda i: (0, 0))],
        out_specs=pl.BlockSpec((tm, d), row),
        out_shape=jax.ShapeDtypeStruct((n, d), F32),
        compiler_params=_params("parallel"),
        name="merge",
    )(x2, *branches, shift, scale, gate, wg, wb, wo, ln_g, ln_b)


def _peer_candidate_blocks(k):
    blocks = []
    for i in range(k):
        need = min(k, (k + 1) // (i + 1))
        if need == 1:
            assert (k - i) % 8 == 0
            blocks.append((i, 1))
            break
        blocks.append((i, -(-need // 8) * 8))
    return tuple(blocks)


_PEER_CAND_BLOCKS = _peer_candidate_blocks(PEER_TOPK)
_PEER_CAND_ROWS = sum((PEER_TOPK - i) if w == 1 else w for i, w in _PEER_CAND_BLOCKS)


def _take_top(ws, count, on_max, with_rank):
    ws = list(ws)
    ranks = [jnp.full(w.shape, float(count), F32) if flag else None for w, flag in zip(ws, with_rank)]
    for r in range(count):
        for n, w in enumerate(ws):
            m = jnp.max(w, axis=0, keepdims=True)
            on_max(n, r, m)
            if r + 1 < count or ranks[n] is not None:
                hit = w == m
                if ranks[n] is not None:
                    ranks[n] = jnp.where(hit, float(r), ranks[n])
                if r + 1 < count:
                    ws[n] = jnp.where(hit, -jnp.inf, w)
    return ranks


def _peer_head_scores(h, keys_ref, q_scr):
    halves = []
    for p in range(2):
        idx = 2 * h + p
        rows = pl.multiple_of(idx * PEER_NKEYS, PEER_NKEYS)
        halves.append(_bdot(keys_ref[idx], q_scr[pl.ds(rows, PEER_NKEYS), :]))
    return halves


def _peer_select_head(h, halves, top_scr, cand_scr, mix_ref, rank_ref, p2_ref):
    k = PEER_TOPK

    def keep(p, r, m):
        top_scr[p, r:r + 1, :] = m

    _, rank = _take_top(halves, k + 1, keep, with_rank=(False, True))
    s1, s2 = halves
    a = top_scr[0, :k, :]
    b = top_scr[1, :k, :]
    row = 0
    for i, width in _PEER_CAND_BLOCKS:
        if width == 1:
            cand_scr[row:row + k - i, :] = a[i:k, :] + b[0:1, :]
            row += k - i
        else:
            cand_scr[row:row + width, :] = a[i:i + 1, :] + b[0:width, :]
            row += width
    best = a[0:1, :] + b[0:1, :]
    stats = {"z": jnp.zeros_like(best)}

    def tally(_, r, m):
        if r < k:
            stats["z"] = stats["z"] + jnp.exp(m - best)
        if r == k - 1:
            stats["last"] = m
        if r == k:
            stats["next"] = m

    _take_top([cand_scr[...]], k + 1, tally, with_rank=(False,))
    runner_up = jnp.maximum(stats["next"], jnp.maximum(top_scr[0, k:k + 1, :] + b[0:1, :],
                                                       a[0:1, :] + top_scr[1, k:k + 1, :]))
    bound = 0.5 * (stats["last"] + runner_up) - s1
    count = jnp.zeros_like(s1)
    for qq in range(k):
        count = jnp.where(b[qq:qq + 1, :] >= bound, float(qq + 1), count)
    mix_ref[h] = count + 0.5 * jnp.exp(s1 - a[0:1, :])
    rank_ref[h] = rank.astype(BF16)
    p2_ref[h] = (jnp.exp(s2 - b[0:1, :]) / stats["z"]).astype(BF16)


def _peer_kernel(x_ref, sh_ref, sc_ref, gf_ref, g_ref, b_ref, wqt_ref, keys_ref, u_ref, vt_ref, o_ref,
                 ub_scr, q_scr, mix_scr, rank_scr, p2_scr, top_scr, cand_scr, w_scr, acc_scr, *,
                 alpha, ipc, sub):
    c = pl.program_id(1)
    nk = PEER_NKEYS

    @pl.when(c == 0)
    def _scores_and_selection():
        u = _norm_rows(x_ref[...]) * (1.0 + sc_ref[...]) + sh_ref[...]
        ub_scr[...] = jnp.transpose(u).astype(BF16)
        q_scr[...] = _bdot(wqt_ref[...], ub_scr[...]).astype(BF16)

        def head(h, carry):
            _peer_select_head(h, _peer_head_scores(h, keys_ref, q_scr), top_scr, cand_scr,
                              mix_scr, rank_scr, p2_scr)
            return carry

        lax.fori_loop(0, PEER_HEADS, head, 0)
        acc_scr[...] = jnp.zeros_like(acc_scr)

    ips = sub // nk
    n_sub = ipc // ips

    def hidden(sb):
        return _bdot(u_ref[sb * sub:(sb + 1) * sub, :], ub_scr[...])

    hids = [hidden(sb) for sb in range(n_sub)]
    partial_out = None
    for sb in range(n_sub):
        rows = slice(sb * sub, (sb + 1) * sub)
        hid = hids[sb]
        for il in range(ips):
            i = c * ipc + sb * ips + il
            gate = None
            for h in range(PEER_HEADS):
                mix_row = mix_scr[h, pl.ds(i, 1), :]
                cnt_row = jnp.floor(mix_row)
                p1_row = ((mix_row - cnt_row) * 2.0).astype(BF16)
                chosen = rank_scr[h] < cnt_row.astype(BF16)
                term = jnp.where(chosen, p2_scr[h], 0.0) * p1_row
                gate = term if gate is None else gate + term
            hb = hid[il * nk:(il + 1) * nk, :]
            act = 0.5 * hb * (1.0 + lax.erf(hb * (2.0 ** -0.5)))
            w_scr[sb * sub + il * nk:sb * sub + (il + 1) * nk, :] = gate * act.astype(BF16)
        out = _bdot(vt_ref[:, rows], w_scr[rows, :])
        partial_out = out if partial_out is None else partial_out + out
    acc_scr[...] += partial_out

    @pl.when(c == pl.num_programs(1) - 1)
    def _finish():
        y = jnp.transpose(acc_scr[...])
        z = alpha * x_ref[...] + gf_ref[...] * y
        o_ref[...] = _norm_rows(z) * g_ref[...] + b_ref[...]


def _peer(x2, shift, scale, gate, ln_g, ln_b, wqt, keys, u_tab, vt_tab, alpha, seq, tt):
    n, d = x2.shape
    chunk, sub = PEER_CHUNK, PEER_SUB
    assert u_tab.shape[0] % chunk == 0 and chunk % sub == 0 and sub % PEER_NKEYS == 0
    spt = seq // tt
    experts = u_tab.shape[0]
    ipc = chunk // PEER_NKEYS
    nhp = 2 * PEER_HEADS
    row = lambda i, c: (i, 0)
    mod = lambda i, c: (i // spt, 0, 0)
    once = pl.Buffered(1)
    return pl.pallas_call(
        functools.partial(_peer_kernel, alpha=alpha, ipc=ipc, sub=sub),
        grid=(n // tt, experts // chunk),
        in_specs=[pl.BlockSpec((tt, d), row)] + [pl.BlockSpec((None, 1, d), mod)] * 3
                 + [pl.BlockSpec((1, d), lambda i, c: (0, 0)), pl.BlockSpec((1, d), lambda i, c: (0, 0)),
                    pl.BlockSpec(wqt.shape, lambda i, c: (0, 0), pipeline_mode=once),
                    pl.BlockSpec(keys.shape, lambda i, c: (0, 0, 0), pipeline_mode=once),
                    pl.BlockSpec((chunk, d), lambda i, c: (c, 0)),
                    pl.BlockSpec((d, chunk), lambda i, c: (0, c))],
        out_specs=pl.BlockSpec((tt, d), row),
        out_shape=jax.ShapeDtypeStruct((n, d), F32),
        scratch_shapes=[pltpu.VMEM((d, tt), BF16),
                        pltpu.VMEM((nhp * PEER_NKEYS, tt), BF16),
                        pltpu.VMEM((PEER_HEADS, PEER_NKEYS, tt), F32),
                        pltpu.VMEM((PEER_HEADS, PEER_NKEYS, tt), BF16),
                        pltpu.VMEM((PEER_HEADS, PEER_NKEYS, tt), BF16),
                        pltpu.VMEM((2, PEER_TOPK + 8, tt), F32),
                        pltpu.VMEM((_PEER_CAND_ROWS, tt), F32),
                        pltpu.VMEM((chunk, tt), BF16),
                        pltpu.VMEM((d, tt), F32)],
        compiler_params=_params("parallel", "arbitrary"),
        name="peer",
    )(x2, shift, scale, gate, ln_g, ln_b, wqt, keys, u_tab, vt_tab)


def _rotate_half_cols(w, dim):
    rows, cols = w.shape
    w3 = w.reshape(rows, cols // dim, dim)
    return jnp.concatenate([-w3[..., dim // 2:], w3[..., :dim // 2]], axis=-1).reshape(rows, cols)


def _pad_cols(w, width):
    return jnp.pad(w, ((0, 0), (0, width - w.shape[1])))


def _in_proj_weights(w_in):
    widths = (512, 128, 128, 384, 256, 32, 512, 512, 512, 512, 512, 512)
    names = ("qa", "ka", "va", "cq", "ckv", "kr", "qc", "kc", "vc", "qd", "kd", "vd")
    parts, off = {}, 0
    for name, width in zip(names, widths):
        parts[name] = w_in[:, off:off + width]
        off += width
    for name, extra in (("qa", 1.0), ("qc", LOG2_E), ("qd", 1.0)):
        parts[name] = parts[name] * (HEAD_DIM ** -0.5 * extra)
    rot_dim = {"qa": HEAD_DIM, "ka": HEAD_DIM, "qc": DIFF_QK, "kc": DIFF_QK, "kr": MLA_ROPE}
    cols = [_pad_cols(parts[name], width) for name, width in _MAIN_GROUPS]
    cols += [_pad_cols(_rotate_half_cols(parts[name], rot_dim[name]), width) for name, width in _ROT_GROUPS]
    return jnp.concatenate(cols, axis=1).astype(BF16)


def _mla_weights(q_norm, q_up, kv_norm, kv_up):
    qr, kvr = q_up.shape[0], kv_up.shape[0]
    qh = q_up.reshape(qr, MLA_HEADS, MLA_NOPE + MLA_ROPE) * ((MLA_NOPE + MLA_ROPE) ** -0.5 * LOG2_E)
    pad = MLA_SLOT - MLA_NOPE - MLA_ROPE
    wq = jnp.pad(qh, ((0, 0), (0, 0), (0, pad))).reshape(qr, MLA_HEADS * MLA_SLOT)
    rot = _rotate_half_cols(qh[..., MLA_NOPE:].reshape(qr, MLA_HEADS * MLA_ROPE), MLA_ROPE)
    rot = rot.reshape(qr, MLA_HEADS, MLA_ROPE)
    wqr = jnp.pad(rot, ((0, 0), (0, 0), (MLA_NOPE, pad))).reshape(qr, MLA_HEADS * MLA_SLOT)
    kvh = kv_up.reshape(kvr, MLA_HEADS, MLA_NOPE + MLA_V)
    wk = jnp.pad(kvh[..., :MLA_NOPE], ((0, 0), (0, 0), (0, MLA_SLOT - MLA_NOPE))).reshape(kvr, MLA_HEADS * MLA_SLOT)
    wv = kvh[..., MLA_NOPE:].reshape(kvr, MLA_HEADS * MLA_V)
    place = np.zeros((128, MLA_HEADS, MLA_SLOT), np.float32)
    for r in range(MLA_ROPE):
        place[r, :, MLA_NOPE + r] = 1.0
    return {"qn": q_norm.reshape(1, qr), "kvn": kv_norm.reshape(1, kvr),
            "wq": wq.astype(BF16), "wqr": wqr.astype(BF16), "wk": wk.astype(BF16), "wv": wv.astype(BF16),
            "e": jnp.asarray(place.reshape(128, MLA_HEADS * MLA_SLOT), BF16)}


def _rope_tables(seq):
    def base(dim):
        inv = ROPE_THETA ** (-jnp.arange(0, dim, 2, dtype=F32) / dim)
        ang = jnp.arange(seq, dtype=F32)[:, None] * inv[None, :]
        cos, sin = jnp.cos(ang), jnp.sin(ang)
        return jnp.concatenate([cos, cos], axis=1), jnp.concatenate([sin, sin], axis=1)

    cos64, sin64 = base(HEAD_DIM)
    cos32, sin32 = base(MLA_ROPE)
    ones = jnp.ones((seq, MLA_NOPE), F32)
    zeros = jnp.zeros((seq, MLA_NOPE), F32)
    tail = jnp.zeros((seq, MLA_SLOT - MLA_NOPE - MLA_ROPE), F32)
    return {"cos64": jnp.concatenate([cos64, cos64], axis=1), "sin64": jnp.concatenate([sin64, sin64], axis=1),
            "cos32": _pad_cols(cos32, 128), "sin32": _pad_cols(sin32, 128),
            "mla_cos": jnp.concatenate([ones, cos32, tail], axis=1),
            "mla_sin": jnp.concatenate([zeros, sin32, tail], axis=1)}


def _tile(total, want):
    t = min(total, want)
    assert total % t == 0
    return t


def kernel(x, c, ada_w, ada_b, w_in, swa_sink, mla_q_norm, mla_q_up, mla_kv_norm, mla_kv_up, diff_lambda_q1, diff_lambda_k1, diff_lambda_q2, diff_lambda_k2, diff_subln, nat_rpb, w_gate, w_branch, w_out, ln1_g, ln1_b, peer_wq, peer_keys, peer_u, peer_v, ln2_g, ln2_b):
    batch, seq, d = x.shape
    depth = ada_w.shape[0]
    alpha = (2 * depth) ** 0.25
    n = batch * seq
    tabs = _rope_tables(seq)
    mod = _ada(c, ada_w, ada_b)
    tm = _tile(seq, 512)
    x2 = x.reshape(n, d)
    for l in range(depth):
        sh_mix, sc_mix, g_mix, sh_ffn, sc_ffn, g_ffn = (mod[l, :, j] for j in range(6))
        (qa, ka, va, cq, ckv, kr, qc, kc, vc, qd, kd, vd) = _in_proj(
            x2, sh_mix, sc_mix, _in_proj_weights(w_in[l]), tabs, seq, tm)
        o_a = _swa(qa, ka, va, swa_sink[l], batch, seq, _tile(seq, SWA_TQ))
        qm, km, vm = _mla_prep(cq, ckv, kr, _mla_weights(mla_q_norm[l], mla_q_up[l], mla_kv_norm[l], mla_kv_up[l]),
                               tabs, seq, tm)
        o_b = _mla(qm, km, vm, batch, seq, _tile(seq, FLASH_TQ), _tile(seq, FLASH_TK))
        lambda_init = 0.8 - 0.6 * math.exp(-0.3 * l)
        lams = tuple(v[l].reshape(1, DIFF_QK) for v in (diff_lambda_q1, diff_lambda_k1, diff_lambda_q2, diff_lambda_k2))
        o_c = _diff(qc, kc, vc, lams, diff_subln[l].reshape(1, DIFF_V), lambda_init, batch, seq,
                    _tile(seq, FLASH_TQ), _tile(seq, FLASH_TK))
        o_d = _nat(qd, kd, vd, nat_rpb[l], batch, seq)
        x2 = _merge(x2, (o_a, o_b, o_c, o_d), sh_mix, sc_mix, g_mix, w_gate[l].astype(BF16),
                    w_branch[l].astype(BF16), w_out[l].astype(BF16), ln1_g[l].reshape(1, d), ln1_b[l].reshape(1, d),
                    alpha, seq, tm)
        keys = peer_keys[l].reshape(2 * PEER_HEADS, PEER_NKEYS, PEER_DKEY // 2).astype(BF16)
        x2 = _peer(x2, sh_ffn, sc_ffn, g_ffn, ln2_g[l].reshape(1, d), ln2_b[l].reshape(1, d),
                   peer_wq[l].T.astype(BF16), keys, peer_u[l].astype(BF16), peer_v[l].T.astype(BF16),
                   alpha, seq, tm)
    return x2.reshape(batch, seq, d)
```

```python
import functools
import math

import jax
import jax.numpy as jnp
import numpy as np
from jax import lax
from jax.experimental import pallas as pl
from jax.experimental.pallas import tpu as pltpu

F32 = jnp.float32
BF16 = jnp.bfloat16

GRID_W = 64
ROPE_THETA = 10000.0
HEAD_DIM = 64
LN_EPS = 1e-5
NEG_INF = -1e30
LOG2_E = 1.4426950408889634

SWA_HEADS = 8
SWA_KV_HEADS = 2
SWA_WINDOW = 128
MLA_HEADS = 8
MLA_NOPE = 64
MLA_ROPE = 32
MLA_V = 64
MLA_SLOT = 128
DIFF_HEADS = 4
DIFF_QK = 64
DIFF_V = 128
NAT_HEADS = 8
NAT_KR = 8
NAT_KC = 16
PEER_HEADS = 8
PEER_NKEYS = 128
PEER_DKEY = 256
PEER_TOPK = 16

VMEM_LIMIT_BYTES = 56 * 1024 * 1024
FLASH_TQ = 2048
FLASH_Q_BLOCKS = 4
FLASH_TK = 1024
SWA_TQ = 256
NAT_ROWS_PER_STEP = 4
PEER_CHUNK = 2048
PEER_SUB = 512

_MAIN_GROUPS = (("qa", 512), ("ka", 128), ("va", 128), ("cq", 384), ("ckv", 256), ("qc", 512), ("kc", 512),
                ("vc", 512), ("qd", 512), ("kd", 512), ("vd", 512), ("kr", 128))
_ROT_GROUPS = (("qa", 512), ("ka", 128), ("qc", 512), ("kc", 512), ("kr", 128))


def _offsets(groups, base=0):
    out, off = {}, base
    for name, width in groups:
        out[name] = (off, off + width)
        off += width
    return out, off


_MAIN_OFF, _MAIN_END = _offsets(_MAIN_GROUPS)
_ROT_OFF, _W_ALL_COLS = _offsets(_ROT_GROUPS, _MAIN_END)

_NT = (((1,), (1,)), ((), ()))


def _params(*sem, fuse_inputs=None):
    return pltpu.CompilerParams(dimension_semantics=sem, vmem_limit_bytes=VMEM_LIMIT_BYTES,
                                allow_input_fusion=fuse_inputs)


def _norm_rows(x):
    mu = jnp.mean(x, axis=-1, keepdims=True)
    xc = x - mu
    var = jnp.mean(xc * xc, axis=-1, keepdims=True)
    return xc * lax.rsqrt(var + LN_EPS)


def _sigmoid(z):
    return 1.0 / (1.0 + jnp.exp(-z))


def _bdot(a, b):
    return jnp.dot(a, b, preferred_element_type=F32)


def _ada_kernel(c_ref, w_ref, b_ref, o_ref):
    c = c_ref[...]
    act = c * _sigmoid(c)
    o_ref[...] = jnp.dot(act, w_ref[...], preferred_element_type=F32,
                         precision=lax.Precision.HIGHEST) + b_ref[...]


def _ada(c, ada_w, ada_b):
    depth, d, six_d = ada_w.shape
    b = c.shape[0]
    rows = -(-b // 8) * 8
    c_pad = jnp.pad(c, ((0, rows - b), (0, 0)))
    out = pl.pallas_call(
        _ada_kernel,
        grid=(depth, six_d // d),
        in_specs=[pl.BlockSpec((rows, d), lambda l, j: (0, 0)),
                  pl.BlockSpec((None, d, d), lambda l, j: (l, 0, j)),
                  pl.BlockSpec((None, 1, d), lambda l, j: (l, 0, j))],
        out_specs=pl.BlockSpec((None, rows, d), lambda l, j: (l, 0, j)),
        out_shape=jax.ShapeDtypeStruct((depth, rows, six_d), F32),
        compiler_params=_params("parallel", "parallel"),
        name="ada",
    )(c_pad, ada_w, ada_b.reshape(depth, 1, six_d))
    return out[:, :b].reshape(depth, b, 6, 1, d)


def _in_proj_kernel(x_ref, sh_ref, sc_ref, w_ref, cos_ref, sin_ref, cos32_ref, sin32_ref,
                    qa_ref, ka_ref, va_ref, cq_ref, ckv_ref, kr_ref, qc_ref, kc_ref, vc_ref,
                    qd_ref, kd_ref, vd_ref):
    u = _norm_rows(x_ref[...]) * (1.0 + sc_ref[...]) + sh_ref[...]
    ub = u.astype(BF16)

    def proj(off):
        return _bdot(ub, w_ref[:, off[0]:off[1]])

    def roped(name, cos, sin):
        main, rot = proj(_MAIN_OFF[name]), proj(_ROT_OFF[name])
        reps = main.shape[1] // cos.shape[1]
        if reps > 1:
            cos = jnp.concatenate([cos] * reps, axis=1)
            sin = jnp.concatenate([sin] * reps, axis=1)
        return main * cos + rot * sin

    cos, sin = cos_ref[...], sin_ref[...]
    qa_ref[...] = roped("qa", cos, sin).astype(qa_ref.dtype)
    ka_ref[...] = roped("ka", cos, sin).astype(ka_ref.dtype)
    va_ref[...] = proj(_MAIN_OFF["va"]).astype(va_ref.dtype)
    cq_ref[...] = proj(_MAIN_OFF["cq"])
    ckv_ref[...] = proj(_MAIN_OFF["ckv"])
    kr_ref[...] = roped("kr", cos32_ref[...], sin32_ref[...]).astype(kr_ref.dtype)
    qc_ref[...] = roped("qc", cos, sin).astype(qc_ref.dtype)
    kc_ref[...] = roped("kc", cos, sin).astype(kc_ref.dtype)
    vc_ref[...] = proj(_MAIN_OFF["vc"]).astype(vc_ref.dtype)
    qd_ref[...] = proj(_MAIN_OFF["qd"]).astype(qd_ref.dtype)
    kd_ref[...] = proj(_MAIN_OFF["kd"]).astype(kd_ref.dtype)
    vd_ref[...] = proj(_MAIN_OFF["vd"]).astype(vd_ref.dtype)


def _in_proj(x2, shift, scale, w_all, tabs, seq, tm):
    n, d = x2.shape
    spt = seq // tm
    widths = dict(_MAIN_GROUPS)
    names = ("qa", "ka", "va", "cq", "ckv", "kr", "qc", "kc", "vc", "qd", "kd", "vd")
    dtypes = {k: BF16 for k in names}
    dtypes["cq"] = F32
    dtypes["ckv"] = F32
    row = lambda i: (i, 0)
    mod = lambda i: (i // spt, 0, 0)
    tab = lambda i: (i % spt, 0)
    return pl.pallas_call(
        _in_proj_kernel,
        grid=(n // tm,),
        in_specs=[pl.BlockSpec((tm, d), row),
                  pl.BlockSpec((None, 1, d), mod),
                  pl.BlockSpec((None, 1, d), mod),
                  pl.BlockSpec((d, _W_ALL_COLS), lambda i: (0, 0), pipeline_mode=pl.Buffered(1)),
                  pl.BlockSpec((tm, 128), tab), pl.BlockSpec((tm, 128), tab),
                  pl.BlockSpec((tm, 128), tab), pl.BlockSpec((tm, 128), tab)],
        out_specs=[pl.BlockSpec((tm, widths[k]), row) for k in names],
        out_shape=[jax.ShapeDtypeStruct((n, widths[k]), dtypes[k]) for k in names],
        compiler_params=_params("parallel"),
        name="in_proj",
    )(x2, shift, scale, w_all, tabs["cos64"], tabs["sin64"], tabs["cos32"], tabs["sin32"])


def _swa_kernel(q_ref, k_ref, v_ref, sink_ref, o_ref, *, tq, seq):
    i = pl.program_id(1)
    span = tq + 2 * SWA_WINDOW
    start = pl.multiple_of(jnp.clip(i * tq - SWA_WINDOW, 0, seq - span), SWA_WINDOW)
    ks = k_ref[pl.ds(start, span), :]
    vs = v_ref[pl.ds(start, span), :]
    qpos = i * tq + lax.broadcasted_iota(jnp.int32, (tq, span), 0)
    kpos = start + lax.broadcasted_iota(jnp.int32, (tq, span), 1)
    valid = jnp.abs(qpos - kpos) <= SWA_WINDOW
    q = q_ref[...]
    grp = SWA_HEADS // SWA_KV_HEADS
    outs = []
    for h in range(SWA_HEADS):
        kv = h // grp
        qh = q[:, h * HEAD_DIM:(h + 1) * HEAD_DIM]
        kh = ks[:, kv * HEAD_DIM:(kv + 1) * HEAD_DIM]
        vh = vs[:, kv * HEAD_DIM:(kv + 1) * HEAD_DIM]
        s = lax.dot_general(qh, kh, _NT, preferred_element_type=F32)
        s = jnp.where(valid, s, NEG_INF)
        sink = sink_ref[h][:, :1]
        m = jnp.maximum(jnp.max(s, axis=-1, keepdims=True), sink)
        p = jnp.exp(s - m)
        denom = jnp.sum(p, axis=-1, keepdims=True) + jnp.exp(sink - m)
        outs.append(_bdot(p.astype(BF16), vh) / denom)
    o_ref[...] = jnp.concatenate(outs, axis=1).astype(o_ref.dtype)


def _swa(qa, ka, va, sink, batch, seq, tq):
    n = qa.shape[0]
    nq = seq // tq
    sink_b = jnp.broadcast_to(sink.astype(F32)[:, None, None], (SWA_HEADS, 1, 128))
    return pl.pallas_call(
        functools.partial(_swa_kernel, tq=tq, seq=seq),
        grid=(batch, nq),
        in_specs=[pl.BlockSpec((tq, 512), lambda b, i: (b * nq + i, 0)),
                  pl.BlockSpec((seq, ka.shape[1]), lambda b, i: (b, 0)),
                  pl.BlockSpec((seq, va.shape[1]), lambda b, i: (b, 0)),
                  pl.BlockSpec((SWA_HEADS, 1, 128), lambda b, i: (0, 0, 0))],
        out_specs=pl.BlockSpec((tq, 512), lambda b, i: (b * nq + i, 0)),
        out_shape=jax.ShapeDtypeStruct((n, 512), BF16),
        compiler_params=_params("parallel", "parallel"),
        name="swa",
    )(qa, ka, va, sink_b)


def _mla_prep_kernel(cq_ref, ckv_ref, kr_ref, qn_ref, kvn_ref, wq_ref, wqr_ref, wk_ref, e_ref, wv_ref,
                     ct_ref, st_ref, qm_ref, km_ref, vm_ref):
    def rms(x, g):
        return (x * lax.rsqrt(jnp.mean(x * x, axis=-1, keepdims=True) + LN_EPS) * g).astype(BF16)

    qn = rms(cq_ref[...], qn_ref[...])
    kvn = rms(ckv_ref[...], kvn_ref[...])
    cos = jnp.concatenate([ct_ref[...]] * MLA_HEADS, axis=1)
    sin = jnp.concatenate([st_ref[...]] * MLA_HEADS, axis=1)
    qm_ref[...] = (_bdot(qn, wq_ref[...]) * cos + _bdot(qn, wqr_ref[...]) * sin).astype(qm_ref.dtype)
    km_ref[...] = (_bdot(kvn, wk_ref[...]) + _bdot(kr_ref[...], e_ref[...])).astype(km_ref.dtype)
    vm_ref[...] = _bdot(kvn, wv_ref[...]).astype(vm_ref.dtype)


def _mla_prep(cq, ckv, kr, w, tabs, seq, tm):
    n = cq.shape[0]
    spt = seq // tm
    row = lambda i: (i, 0)
    full = lambda i: (0, 0)
    tab = lambda i: (i % spt, 0)
    hw = MLA_HEADS * MLA_SLOT
    return pl.pallas_call(
        _mla_prep_kernel,
        grid=(n // tm,),
        in_specs=[pl.BlockSpec((tm, cq.shape[1]), row), pl.BlockSpec((tm, ckv.shape[1]), row),
                  pl.BlockSpec((tm, 128), row),
                  pl.BlockSpec((1, cq.shape[1]), full), pl.BlockSpec((1, ckv.shape[1]), full),
                  pl.BlockSpec(w["wq"].shape, full), pl.BlockSpec(w["wqr"].shape, full),
                  pl.BlockSpec(w["wk"].shape, full), pl.BlockSpec(w["e"].shape, full),
                  pl.BlockSpec(w["wv"].shape, full),
                  pl.BlockSpec((tm, 128), tab), pl.BlockSpec((tm, 128), tab)],
        out_specs=[pl.BlockSpec((tm, hw), row), pl.BlockSpec((tm, hw), row),
                   pl.BlockSpec((tm, MLA_HEADS * MLA_V), row)],
        out_shape=[jax.ShapeDtypeStruct((n, hw), BF16), jax.ShapeDtypeStruct((n, hw), BF16),
                   jax.ShapeDtypeStruct((n, MLA_HEADS * MLA_V), BF16)],
        compiler_params=_params("parallel"),
        name="mla_prep",
    )(cq, ckv, kr, w["qn"], w["kvn"], w["wq"], w["wqr"], w["wk"], w["e"], w["wv"],
      tabs["mla_cos"], tabs["mla_sin"])


def _flash_chains(qs, k_ref, vt_ref, k_lanes, seq, tk, with_ones, kinds, ones_row):
    n_rows = [v.shape[0] for v in jax.eval_shape(with_ones, jax.ShapeDtypeStruct(vt_ref.shape[1:], vt_ref.dtype))]

    def step(kt, carry):
        off = pl.multiple_of(kt * tk, tk)
        variants = with_ones(vt_ref[kt])
        out = []
        scores = [lax.dot_general(k_ref[pl.ds(off, tk), lanes[0]:lanes[1]], q, _NT, preferred_element_type=F32)
                  for q, lanes in zip(qs, k_lanes)]
        for s, kind, (m, acc) in zip(scores, kinds, carry):
            m_new = jnp.maximum(m, jnp.max(s, axis=0, keepdims=True))
            alpha = jnp.exp2(m - m_new)
            p = jnp.exp2(s - m_new).astype(BF16)
            acc = alpha * acc + _bdot(variants[kind], p)
            out.append((m_new, acc))
        return tuple(out)

    init = tuple((jnp.full((1, q.shape[0]), NEG_INF, F32), jnp.zeros((n_rows[kind], q.shape[0]), F32))
                 for q, kind in zip(qs, kinds))
    final = lax.fori_loop(0, seq // tk, step, init)
    return [(acc, acc[row:row + 1, :]) for (_, acc), row in zip(final, ones_row)]


def _values_transposed(v, batch, seq, tk):
    groups = v.shape[1] // 128
    v5 = v.reshape(batch, seq // tk, tk, groups, 128)
    return jnp.transpose(v5, (0, 3, 1, 4, 2))


def _mla_kernel(q_ref, k_ref, vt_ref, o_ref, *, seq, tk):
    rows = q_ref.shape[0] // FLASH_Q_BLOCKS
    lanes = [(hh * MLA_SLOT, (hh + 1) * MLA_SLOT) for hh in range(2)]
    qs = [q_ref[r * rows:(r + 1) * rows, lo:hi] for r in range(FLASH_Q_BLOCKS) for lo, hi in lanes]

    def with_ones(vt):
        vrow = lax.broadcasted_iota(jnp.int32, vt.shape, 0)
        one = jnp.ones_like(vt)
        return [jnp.where(vrow < MLA_V, vt, one), jnp.where(vrow < MLA_V, one, vt)]

    res = _flash_chains(qs, k_ref, vt_ref, lanes * FLASH_Q_BLOCKS, seq, tk, with_ones,
                        kinds=[0, 1] * FLASH_Q_BLOCKS, ones_row=[MLA_V, 0] * FLASH_Q_BLOCKS)
    outs = [acc / l for acc, l in res]
    row = lax.broadcasted_iota(jnp.int32, outs[0].shape, 0)
    for r in range(FLASH_Q_BLOCKS):
        o_t = jnp.where(row < MLA_V, outs[2 * r], outs[2 * r + 1])
        o_ref[r * rows:(r + 1) * rows, :] = jnp.transpose(o_t).astype(o_ref.dtype)


def _mla(qm, km, vm, batch, seq, tq, tk):
    n = qm.shape[0]
    nq = seq // tq
    pairs = MLA_HEADS // 2
    return pl.pallas_call(
        functools.partial(_mla_kernel, seq=seq, tk=tk),
        grid=(batch, pairs, nq),
        in_specs=[pl.BlockSpec((tq, 2 * MLA_SLOT), lambda b, h, i: (b * nq + i, h)),
                  pl.BlockSpec((seq, 2 * MLA_SLOT), lambda b, h, i: (b, h)),
                  pl.BlockSpec((None, None, seq // tk, 2 * MLA_V, tk), lambda b, h, i: (b, h, 0, 0, 0))],
        out_specs=pl.BlockSpec((tq, 2 * MLA_V), lambda b, h, i: (b * nq + i, h)),
        out_shape=jax.ShapeDtypeStruct((n, MLA_HEADS * MLA_V), BF16),
        compiler_params=_params("parallel", "parallel", "parallel"),
        name="mla",
    )(qm, km, _values_transposed(vm, batch, seq, tk))


def _diff_kernel(q_ref, k_ref, vt_ref, lq1_ref, lk1_ref, lq2_ref, lk2_ref, sub_ref, o_ref, *,
                 seq, tk, lambda_init):
    rows = q_ref.shape[0] // FLASH_Q_BLOCKS
    lane = lax.broadcasted_iota(jnp.int32, (rows, 2 * DIFF_QK), 1)
    qs = []
    for r in range(FLASH_Q_BLOCKS):
        q = q_ref[r * rows:(r + 1) * rows, :]
        zero = jnp.zeros_like(q)
        qs += [jnp.where(lane < DIFF_QK, q, zero), jnp.where(lane >= DIFF_QK, q, zero)]
    def with_ones(vt):
        return [jnp.concatenate([vt, jnp.ones((16, vt.shape[1]), vt.dtype)], axis=0)]

    res = _flash_chains(qs, k_ref, vt_ref, [(0, 2 * DIFF_QK)] * len(qs), seq, tk, with_ones,
                        kinds=[0] * len(qs), ones_row=[DIFF_V] * len(qs))
    lam = (jnp.exp(jnp.sum(lq1_ref[...] * lk1_ref[...], axis=-1, keepdims=True))
           - jnp.exp(jnp.sum(lq2_ref[...] * lk2_ref[...], axis=-1, keepdims=True)) + lambda_init)
    for r in range(FLASH_Q_BLOCKS):
        (acc1, l1), (acc2, l2) = res[2 * r], res[2 * r + 1]
        o = jnp.transpose(acc1[:DIFF_V] / l1 - lam * (acc2[:DIFF_V] / l2))
        o = o * lax.rsqrt(jnp.mean(o * o, axis=-1, keepdims=True) + LN_EPS) * sub_ref[...]
        o_ref[r * rows:(r + 1) * rows, :] = (o * (1.0 - lambda_init)).astype(o_ref.dtype)


def _diff(qc, kc, vc, lams, subln, lambda_init, batch, seq, tq, tk):
    n = qc.shape[0]
    nq = seq // tq
    vec = lambda b, h, i: (0, 0)
    return pl.pallas_call(
        functools.partial(_diff_kernel, seq=seq, tk=tk, lambda_init=lambda_init),
        grid=(batch, DIFF_HEADS, nq),
        in_specs=[pl.BlockSpec((tq, 2 * DIFF_QK), lambda b, h, i: (b * nq + i, h)),
                  pl.BlockSpec((seq, 2 * DIFF_QK), lambda b, h, i: (b, h)),
                  pl.BlockSpec((None, None, seq // tk, DIFF_V, tk), lambda b, h, i: (b, h, 0, 0, 0)),
                  pl.BlockSpec((1, DIFF_QK), vec), pl.BlockSpec((1, DIFF_QK), vec),
                  pl.BlockSpec((1, DIFF_QK), vec), pl.BlockSpec((1, DIFF_QK), vec),
                  pl.BlockSpec((1, DIFF_V), vec)],
        out_specs=pl.BlockSpec((tq, DIFF_V), lambda b, h, i: (b * nq + i, h)),
        out_shape=jax.ShapeDtypeStruct((n, DIFF_HEADS * DIFF_V), BF16),
        compiler_params=_params("parallel", "parallel", "parallel"),
        name="diff",
    )(qc, kc, _values_transposed(vc, batch, seq, tk), *lams, subln)


def _nat_row_start(r, rows):
    return jnp.clip(r - NAT_KR // 2, 0, rows - NAT_KR)


def _nat_kernel(q_ref, k_ref, v_ref, bias_ref, o_ref, *, rows, rows_per_step):
    pairs = [slice(pp * 2 * HEAD_DIM, (pp + 1) * 2 * HEAD_DIM) for pp in range(NAT_HEADS // 2)]
    lane = lax.broadcasted_iota(jnp.int32, (GRID_W, 2 * HEAD_DIM), 1)
    values, logits = [], []
    for rr in range(rows_per_step):
        r = pl.program_id(1) * rows_per_step + rr
        r0 = _nat_row_start(r, rows)
        band = r0 - r + (NAT_KR - 1)
        off = pl.multiple_of(r0 * GRID_W, GRID_W)
        ks = k_ref[pl.ds(off, NAT_KR * GRID_W), :]
        values.append(v_ref[pl.ds(off, NAT_KR * GRID_W), :])
        q = q_ref[rr * GRID_W:(rr + 1) * GRID_W, :]
        row_logits = []
        for pp, sl in enumerate(pairs):
            qp = q[:, sl]
            zero = jnp.zeros_like(qp)
            for half, qh in enumerate((jnp.where(lane < HEAD_DIM, qp, zero), jnp.where(lane < HEAD_DIM, zero, qp))):
                row_logits.append(lax.dot_general(qh, ks[:, sl], _NT, preferred_element_type=F32)
                                  + bias_ref[band, 2 * pp + half])
        logits.append(row_logits)
    probs = []
    for row_logits in logits:
        row_probs = []
        for s in row_logits:
            p = jnp.exp(s - jnp.max(s, axis=-1, keepdims=True))
            row_probs.append((p.astype(BF16), jnp.sum(p, axis=-1, keepdims=True)))
        probs.append(row_probs)
    for rr, (row_probs, vs) in enumerate(zip(probs, values)):
        outs = []
        for pp, sl in enumerate(pairs):
            (p0, d0), (p1, d1) = row_probs[2 * pp], row_probs[2 * pp + 1]
            outs.append(jnp.where(lane < HEAD_DIM, _bdot(p0, vs[:, sl]) / d0, _bdot(p1, vs[:, sl]) / d1))
        o_ref[rr * GRID_W:(rr + 1) * GRID_W, :] = jnp.concatenate(outs, axis=1).astype(o_ref.dtype)


def _nat_bias_table(rpb):
    heads, n_row_off, n_col_off = rpb.shape
    cols = np.arange(GRID_W)
    col_start = np.clip(cols - NAT_KC // 2, 0, GRID_W - NAT_KC)
    inside = (cols[None, :] >= col_start[:, None]) & (cols[None, :] < col_start[:, None] + NAT_KC)
    period = 2 * GRID_W
    lead = GRID_W - NAT_KC
    sig = jnp.pad(rpb.astype(F32), ((0, 0), (0, 0), (lead, period - n_col_off - lead)))
    flat = jnp.tile(sig, (1, 1, GRID_W))[..., :GRID_W * (period - 1)]
    toep = flat.reshape(heads, n_row_off, GRID_W, period - 1)[..., GRID_W - 1:]
    toep = jnp.where(inside[None, None], toep, NEG_INF)
    bands = [jnp.transpose(toep[:, d:d + NAT_KR], (0, 2, 1, 3)).reshape(heads, GRID_W, NAT_KR * GRID_W)
             for d in range(NAT_KR)]
    return jnp.stack(bands, axis=0)


def _nat(qd, kd, vd, rpb, batch, seq):
    n = qd.shape[0]
    rows = seq // GRID_W
    assert rows >= NAT_KR
    bias = _nat_bias_table(rpb)
    rps = NAT_ROWS_PER_STEP
    assert rows % rps == 0
    steps = rows // rps
    once = pl.Buffered(1)
    return pl.pallas_call(
        functools.partial(_nat_kernel, rows=rows, rows_per_step=rps),
        grid=(batch, steps),
        in_specs=[pl.BlockSpec((rps * GRID_W, 512), lambda b, r: (b * steps + r, 0)),
                  pl.BlockSpec((seq, 512), lambda b, r: (b, 0), pipeline_mode=once),
                  pl.BlockSpec((seq, 512), lambda b, r: (b, 0), pipeline_mode=once),
                  pl.BlockSpec(bias.shape, lambda b, r: (0, 0, 0, 0), pipeline_mode=once)],
        out_specs=pl.BlockSpec((rps * GRID_W, 512), lambda b, r: (b * steps + r, 0)),
        out_shape=jax.ShapeDtypeStruct((n, 512), BF16),
        compiler_params=_params("parallel", "arbitrary"),
        name="nat",
    )(qd, kd, vd, bias)


def _merge_kernel(x_ref, oa_ref, ob_ref, oc_ref, od_ref, sh_ref, sc_ref, gm_ref, wg_ref, wb_ref, wo_ref,
                  g_ref, b_ref, o_ref, *, alpha):
    x = x_ref[...]
    ub = (_norm_rows(x) * (1.0 + sc_ref[...]) + sh_ref[...]).astype(BF16)
    merged = None
    for n, br_ref in enumerate((oa_ref, ob_ref, oc_ref, od_ref)):
        term = _sigmoid(_bdot(ub, wg_ref[n])) * _bdot(br_ref[...], wb_ref[n])
        merged = term if merged is None else merged + term
    y = _bdot(merged.astype(BF16), wo_ref[...])
    z = alpha * x + gm_ref[...] * y
    o_ref[...] = _norm_rows(z) * g_ref[...] + b_ref[...]


def _merge(x2, branches, shift, scale, gate, wg, wb, wo, ln_g, ln_b, alpha, seq, tm):
    n, d = x2.shape
    spt = seq // tm
    row = lambda i: (i, 0)
    mod = lambda i: (i // spt, 0, 0)
    once = pl.Buffered(1)
    return pl.pallas_call(
        functools.partial(_merge_kernel, alpha=alpha),
        grid=(n // tm,),
        in_specs=[pl.BlockSpec((tm, d), row)] + [pl.BlockSpec((tm, 512), row)] * 4
                 + [pl.BlockSpec((None, 1, d), mod)] * 3
                 + [pl.BlockSpec(wg.shape, lambda i: (0, 0, 0), pipeline_mode=once),
                    pl.BlockSpec(wb.shape, lambda i: (0, 0, 0), pipeline_mode=once),
                    pl.BlockSpec(wo.shape, lambda i: (0, 0), pipeline_mode=once),
                    pl.BlockSpec((1, d), lambda i: (0, 0)), pl.BlockSpec((1, d), lambda i: (0, 0))],
        out_specs=pl.BlockSpec((tm, d), row),
        out_shape=jax.ShapeDtypeStruct((n, d), F32),
        compiler_params=_params("parallel", fuse_inputs=[False] * 8 + [True] * 3 + [False] * 2),
        name="merge",
    )(x2, *branches, shift, scale, gate, wg, wb, wo, ln_g, ln_b)


def _peer_candidate_blocks(k):
    blocks = []
    for i in range(k):
        need = min(k, (k + 1) // (i + 1))
        if need == 1:
            assert (k - i) % 8 == 0
            blocks.append((i, 1))
            break
        blocks.append((i, -(-need // 8) * 8))
    return tuple(blocks)


_PEER_CAND_BLOCKS = _peer_candidate_blocks(PEER_TOPK)
_PEER_CAND_ROWS = sum((PEER_TOPK - i) if w == 1 else w for i, w in _PEER_CAND_BLOCKS)


def _take_top(ws, count, on_max, with_rank):
    ws = list(ws)
    ranks = [jnp.full(w.shape, float(count), F32) if flag else None for w, flag in zip(ws, with_rank)]
    for r in range(count):
        for n, w in enumerate(ws):
            m = jnp.max(w, axis=0, keepdims=True)
            on_max(n, r, m)
            if r + 1 < count or ranks[n] is not None:
                hit = w == m
                if ranks[n] is not None:
                    ranks[n] = jnp.where(hit, float(r), ranks[n])
                if r + 1 < count:
                    ws[n] = jnp.where(hit, -jnp.inf, w)
    return ranks


def _peer_head_scores(h, keys_ref, q_scr):
    halves = []
    for p in range(2):
        idx = 2 * h + p
        rows = pl.multiple_of(idx * PEER_NKEYS, PEER_NKEYS)
        halves.append(_bdot(keys_ref[idx], q_scr[pl.ds(rows, PEER_NKEYS), :]))
    return halves


def _peer_select_head(h, halves, top_scr, cand_scr, mix_ref, rank_ref, p2_ref):
    k = PEER_TOPK

    def keep(p, r, m):
        top_scr[p, r:r + 1, :] = m

    _, rank = _take_top(halves, k + 1, keep, with_rank=(False, True))
    s1, s2 = halves
    a = top_scr[0, :k, :]
    b = top_scr[1, :k, :]
    row = 0
    for i, width in _PEER_CAND_BLOCKS:
        if width == 1:
            cand_scr[row:row + k - i, :] = a[i:k, :] + b[0:1, :]
            row += k - i
        else:
            cand_scr[row:row + width, :] = a[i:i + 1, :] + b[0:width, :]
            row += width
    best = a[0:1, :] + b[0:1, :]
    stats = {"z": jnp.zeros_like(best)}

    def tally(_, r, m):
        if r < k:
            stats["z"] = stats["z"] + jnp.exp(m - best)
        if r == k - 1:
            stats["last"] = m
        if r == k:
            stats["next"] = m

    _take_top([cand_scr[...]], k + 1, tally, with_rank=(False,))
    runner_up = jnp.maximum(stats["next"], jnp.maximum(top_scr[0, k:k + 1, :] + b[0:1, :],
                                                       a[0:1, :] + top_scr[1, k:k + 1, :]))
    bound = 0.5 * (stats["last"] + runner_up) - s1
    count = jnp.zeros_like(s1)
    for qq in range(k):
        count = jnp.where(b[qq:qq + 1, :] >= bound, float(qq + 1), count)
    mix_ref[h] = count + 0.5 * jnp.exp(s1 - a[0:1, :])
    rank_ref[h] = rank.astype(BF16)
    p2_ref[h] = (jnp.exp(s2 - b[0:1, :]) / stats["z"]).astype(BF16)


def _peer_kernel(x_ref, xn_ref, sh_ref, sc_ref, gf_ref, g_ref, b_ref, wqt_ref, keys_ref, u_ref, vt_ref, o_ref,
                 ub_scr, q_scr, mix_scr, rank_scr, p2_scr, top_scr, cand_scr, w_scr, acc_scr, *,
                 alpha, ipc, sub):
    g = pl.program_id(0)
    c = pl.program_id(1)
    nxt = g % 2
    cur = 1 - nxt
    nk = PEER_NKEYS
    k = PEER_TOPK

    @pl.when(jnp.logical_and(g == 0, c == 0))
    def _zero_first_slot():
        ub_scr[1] = jnp.zeros(ub_scr.shape[1:], ub_scr.dtype)
        mix_scr[1] = jnp.zeros(mix_scr.shape[1:], mix_scr.dtype)
        rank_scr[1] = jnp.zeros(rank_scr.shape[1:], rank_scr.dtype)
        p2_scr[1] = jnp.zeros(p2_scr.shape[1:], p2_scr.dtype)

    @pl.when(c == 0)
    def _next_tile_queries():
        u = _norm_rows(xn_ref[...]) * (1.0 + sc_ref[...]) + sh_ref[...]
        ub_scr[nxt] = jnp.transpose(u).astype(BF16)
        q_scr[...] = _bdot(wqt_ref[...], ub_scr[nxt]).astype(BF16)
        acc_scr[...] = jnp.zeros_like(acc_scr)

    ips = sub // nk
    n_sub = ipc // ips

    def hidden(sb):
        return _bdot(u_ref[sb * sub:(sb + 1) * sub, :], ub_scr[cur])

    halves = _peer_head_scores(c, keys_ref, q_scr)
    hids = [hidden(sb) for sb in range(n_sub)]
    _peer_select_head(c, halves, top_scr, cand_scr, mix_scr.at[nxt], rank_scr.at[nxt], p2_scr.at[nxt])

    partial_out = None
    for sb in range(n_sub):
        rows = slice(sb * sub, (sb + 1) * sub)
        hid = hids[sb]
        for il in range(ips):
            i = c * ipc + sb * ips + il
            gate = None
            for h in range(PEER_HEADS):
                mix_row = mix_scr[cur, h, pl.ds(i, 1), :]
                cnt_row = jnp.floor(mix_row)
                p1_row = ((mix_row - cnt_row) * 2.0).astype(BF16)
                chosen = rank_scr[cur, h] < cnt_row.astype(BF16)
                term = jnp.where(chosen, p2_scr[cur, h], 0.0) * p1_row
                gate = term if gate is None else gate + term
            hb = hid[il * nk:(il + 1) * nk, :]
            act = 0.5 * hb * (1.0 + lax.erf(hb * (2.0 ** -0.5)))
            w_scr[sb * sub + il * nk:sb * sub + (il + 1) * nk, :] = gate * act.astype(BF16)
        out = _bdot(vt_ref[:, rows], w_scr[rows, :])
        partial_out = out if partial_out is None else partial_out + out
    acc_scr[...] += partial_out

    @pl.when(jnp.logical_and(c == pl.num_programs(1) - 1, g > 0))
    def _finish():
        y = jnp.transpose(acc_scr[...])
        z = alpha * x_ref[...] + gf_ref[...] * y
        o_ref[...] = _norm_rows(z) * g_ref[...] + b_ref[...]


def _peer(x2, shift, scale, gate, ln_g, ln_b, wqt, keys, u_tab, vt_tab, alpha, seq, tt):
    n, d = x2.shape
    chunk, sub = PEER_CHUNK, PEER_SUB
    assert u_tab.shape[0] % chunk == 0 and chunk % sub == 0 and sub % PEER_NKEYS == 0
    spt = seq // tt
    experts = u_tab.shape[0]
    ipc = chunk // PEER_NKEYS
    nhp = 2 * PEER_HEADS
    n_tiles = n // tt
    assert experts // chunk == PEER_HEADS
    prev = lambda g: jnp.maximum(g - 1, 0)
    this = lambda g: jnp.minimum(g, n_tiles - 1)
    once = pl.Buffered(1)
    return pl.pallas_call(
        functools.partial(_peer_kernel, alpha=alpha, ipc=ipc, sub=sub),
        grid=(n_tiles + 1, experts // chunk),
        in_specs=[pl.BlockSpec((tt, d), lambda g, c: (prev(g), 0), pipeline_mode=once),
                  pl.BlockSpec((tt, d), lambda g, c: (this(g), 0), pipeline_mode=once),
                  pl.BlockSpec((None, 1, d), lambda g, c: (this(g) // spt, 0, 0)),
                  pl.BlockSpec((None, 1, d), lambda g, c: (this(g) // spt, 0, 0)),
                  pl.BlockSpec((None, 1, d), lambda g, c: (prev(g) // spt, 0, 0)),
                  pl.BlockSpec((1, d), lambda g, c: (0, 0)), pl.BlockSpec((1, d), lambda g, c: (0, 0)),
                  pl.BlockSpec(wqt.shape, lambda g, c: (0, 0), pipeline_mode=once),
                  pl.BlockSpec(keys.shape, lambda g, c: (0, 0, 0), pipeline_mode=once),
                  pl.BlockSpec((chunk, d), lambda g, c: (c, 0)),
                  pl.BlockSpec((d, chunk), lambda g, c: (0, c))],
        out_specs=pl.BlockSpec((tt, d), lambda g, c: (prev(g), 0)),
        out_shape=jax.ShapeDtypeStruct((n, d), F32),
        scratch_shapes=[pltpu.VMEM((2, d, tt), BF16),
                        pltpu.VMEM((nhp * PEER_NKEYS, tt), BF16),
                        pltpu.VMEM((2, PEER_HEADS, PEER_NKEYS, tt), F32),
                        pltpu.VMEM((2, PEER_HEADS, PEER_NKEYS, tt), BF16),
                        pltpu.VMEM((2, PEER_HEADS, PEER_NKEYS, tt), BF16),
                        pltpu.VMEM((2, PEER_TOPK + 8, tt), F32),
                        pltpu.VMEM((_PEER_CAND_ROWS, tt), F32),
                        pltpu.VMEM((chunk, tt), BF16),
                        pltpu.VMEM((d, tt), F32)],
        compiler_params=_params("arbitrary", "arbitrary", fuse_inputs=[False] * 8 + [True] * 2 + [False]),
        name="peer",
    )(x2, x2, shift, scale, gate, ln_g, ln_b, wqt, keys, u_tab, vt_tab)


def _rotate_half_cols(w, dim):
    rows, cols = w.shape
    w3 = w.reshape(rows, cols // dim, dim)
    return jnp.concatenate([-w3[..., dim // 2:], w3[..., :dim // 2]], axis=-1).reshape(rows, cols)


def _pad_cols(w, width):
    return jnp.pad(w, ((0, 0), (0, width - w.shape[1])))


def _in_proj_weights(w_in):
    widths = (512, 128, 128, 384, 256, 32, 512, 512, 512, 512, 512, 512)
    names = ("qa", "ka", "va", "cq", "ckv", "kr", "qc", "kc", "vc", "qd", "kd", "vd")
    parts, off = {}, 0
    for name, width in zip(names, widths):
        parts[name] = w_in[:, off:off + width]
        off += width
    for name, extra in (("qa", 1.0), ("qc", LOG2_E), ("qd", 1.0)):
        parts[name] = parts[name] * (HEAD_DIM ** -0.5 * extra)
    rot_dim = {"qa": HEAD_DIM, "ka": HEAD_DIM, "qc": DIFF_QK, "kc": DIFF_QK, "kr": MLA_ROPE}
    cols = [_pad_cols(parts[name], width) for name, width in _MAIN_GROUPS]
    cols += [_pad_cols(_rotate_half_cols(parts[name], rot_dim[name]), width) for name, width in _ROT_GROUPS]
    return jnp.concatenate(cols, axis=1).astype(BF16)


def _mla_weights(q_norm, q_up, kv_norm, kv_up):
    qr, kvr = q_up.shape[0], kv_up.shape[0]
    qh = q_up.reshape(qr, MLA_HEADS, MLA_NOPE + MLA_ROPE) * ((MLA_NOPE + MLA_ROPE) ** -0.5 * LOG2_E)
    pad = MLA_SLOT - MLA_NOPE - MLA_ROPE
    wq = jnp.pad(qh, ((0, 0), (0, 0), (0, pad))).reshape(qr, MLA_HEADS * MLA_SLOT)
    rot = _rotate_half_cols(qh[..., MLA_NOPE:].reshape(qr, MLA_HEADS * MLA_ROPE), MLA_ROPE)
    rot = rot.reshape(qr, MLA_HEADS, MLA_ROPE)
    wqr = jnp.pad(rot, ((0, 0), (0, 0), (MLA_NOPE, pad))).reshape(qr, MLA_HEADS * MLA_SLOT)
    kvh = kv_up.reshape(kvr, MLA_HEADS, MLA_NOPE + MLA_V)
    wk = jnp.pad(kvh[..., :MLA_NOPE], ((0, 0), (0, 0), (0, MLA_SLOT - MLA_NOPE))).reshape(kvr, MLA_HEADS * MLA_SLOT)
    wv = kvh[..., MLA_NOPE:].reshape(kvr, MLA_HEADS * MLA_V)
    place = np.zeros((128, MLA_HEADS, MLA_SLOT), np.float32)
    for r in range(MLA_ROPE):
        place[r, :, MLA_NOPE + r] = 1.0
    return {"qn": q_norm.reshape(1, qr), "kvn": kv_norm.reshape(1, kvr),
            "wq": wq.astype(BF16), "wqr": wqr.astype(BF16), "wk": wk.astype(BF16), "wv": wv.astype(BF16),
            "e": jnp.asarray(place.reshape(128, MLA_HEADS * MLA_SLOT), BF16)}


def _rope_tables(seq):
    def base(dim):
        inv = ROPE_THETA ** (-jnp.arange(0, dim, 2, dtype=F32) / dim)
        ang = jnp.arange(seq, dtype=F32)[:, None] * inv[None, :]
        cos, sin = jnp.cos(ang), jnp.sin(ang)
        return jnp.concatenate([cos, cos], axis=1), jnp.concatenate([sin, sin], axis=1)

    cos64, sin64 = base(HEAD_DIM)
    cos32, sin32 = base(MLA_ROPE)
    ones = jnp.ones((seq, MLA_NOPE), F32)
    zeros = jnp.zeros((seq, MLA_NOPE), F32)
    tail = jnp.zeros((seq, MLA_SLOT - MLA_NOPE - MLA_ROPE), F32)
    return {"cos64": jnp.concatenate([cos64, cos64], axis=1), "sin64": jnp.concatenate([sin64, sin64], axis=1),
            "cos32": _pad_cols(cos32, 128), "sin32": _pad_cols(sin32, 128),
            "mla_cos": jnp.concatenate([ones, cos32, tail], axis=1),
            "mla_sin": jnp.concatenate([zeros, sin32, tail], axis=1)}


def _tile(total, want):
    t = min(total, want)
    assert total % t == 0
    return t


def kernel(x, c, ada_w, ada_b, w_in, swa_sink, mla_q_norm, mla_q_up, mla_kv_norm, mla_kv_up, diff_lambda_q1, diff_lambda_k1, diff_lambda_q2, diff_lambda_k2, diff_subln, nat_rpb, w_gate, w_branch, w_out, ln1_g, ln1_b, peer_wq, peer_keys, peer_u, peer_v, ln2_g, ln2_b):
    batch, seq, d = x.shape
    depth = ada_w.shape[0]
    alpha = (2 * depth) ** 0.25
    n = batch * seq
    tabs = _rope_tables(seq)
    mod = _ada(c, ada_w, ada_b)
    tm = _tile(seq, 512)
    x2 = x.reshape(n, d)
    for l in range(depth):
        sh_mix, sc_mix, g_mix, sh_ffn, sc_ffn, g_ffn = (mod[l, :, j] for j in range(6))
        (qa, ka, va, cq, ckv, kr, qc, kc, vc, qd, kd, vd) = _in_proj(
            x2, sh_mix, sc_mix, _in_proj_weights(w_in[l]), tabs, seq, tm)
        o_a = _swa(qa, ka, va, swa_sink[l], batch, seq, _tile(seq, SWA_TQ))
        qm, km, vm = _mla_prep(cq, ckv, kr, _mla_weights(mla_q_norm[l], mla_q_up[l], mla_kv_norm[l], mla_kv_up[l]),
                               tabs, seq, tm)
        o_b = _mla(qm, km, vm, batch, seq, _tile(seq, FLASH_TQ), _tile(seq, FLASH_TK))
        lambda_init = 0.8 - 0.6 * math.exp(-0.3 * l)
        lams = tuple(v[l].reshape(1, DIFF_QK) for v in (diff_lambda_q1, diff_lambda_k1, diff_lambda_q2, diff_lambda_k2))
        o_c = _diff(qc, kc, vc, lams, diff_subln[l].reshape(1, DIFF_V), lambda_init, batch, seq,
                    _tile(seq, FLASH_TQ), _tile(seq, FLASH_TK))
        o_d = _nat(qd, kd, vd, nat_rpb[l], batch, seq)
        x2 = _merge(x2, (o_a, o_b, o_c, o_d), sh_mix, sc_mix, g_mix, w_gate[l].astype(BF16),
                    w_branch[l].astype(BF16), w_out[l].astype(BF16), ln1_g[l].reshape(1, d), ln1_b[l].reshape(1, d),
                    alpha, seq, tm)
        keys = peer_keys[l].reshape(2 * PEER_HEADS, PEER_NKEYS, PEER_DKEY // 2).astype(BF16)
        x2 = _peer(x2, sh_ffn, sc_ffn, g_ffn, ln2_g[l].reshape(1, d), ln2_b[l].reshape(1, d),
                   peer_wq[l].T.astype(BF16), keys, peer_u[l].astype(BF16), peer_v[l].T.astype(BF16),
                   alpha, seq, tm)
    return x2.reshape(batch, seq, d)
```
